```python
import math
import jax, jax.numpy as jnp
from jax import lax
import numpy as np

D_MODEL = 1024
BATCH = 16
SEQ = 4096
DEPTH = 1

HEAD_DIM = 64
N_MOBA_HEADS = 8
N_FOX_HEADS = 8
MOBA_WIDTH = N_MOBA_HEADS * HEAD_DIM
FOX_WIDTH = N_FOX_HEADS * HEAD_DIM
MOBA_BLOCK = 256
MOBA_TOPK = 3
Q_CHUNK = 128
ROPE_THETA = 500000.0
ROPE_DIM = HEAD_DIM // 4
D_FF = 11 * D_MODEL // 4
CONV_WIDTH = 3
NORM_EPS = 1e-6
N_BRANCH = 2
NEG_INF = -1e30
IN_SPLITS = (MOBA_WIDTH, MOBA_WIDTH, MOBA_WIDTH, FOX_WIDTH, FOX_WIDTH, FOX_WIDTH, N_FOX_HEADS, N_BRANCH * D_MODEL)
IN_COLS = sum(IN_SPLITS)

kernel_name = "hybrid_moba_fox_gated_convffn"


def rms_norm(x, g):
    xf = x.astype(jnp.float32)
    y = xf * lax.rsqrt(jnp.mean(xf * xf, axis=-1, keepdims=True) + NORM_EPS)
    return (y * g.astype(jnp.float32)).astype(x.dtype)


def split_heads(t, n_heads):
    b, s, _ = t.shape
    return t.reshape(b, s, n_heads, HEAD_DIM).transpose(0, 2, 1, 3)


def merge_heads(t):
    b, h, s, d = t.shape
    return t.transpose(0, 2, 1, 3).reshape(b, s, h * d)


def partial_rope(x, pos):
    half = ROPE_DIM // 2
    inv_freq = jnp.power(ROPE_THETA, -jnp.arange(half, dtype=jnp.float32) * 2.0 / ROPE_DIM)
    ang = pos.astype(jnp.float32)[:, None] * inv_freq[None, :]
    cos, sin = jnp.cos(ang), jnp.sin(ang)
    xr = x[..., :ROPE_DIM].astype(jnp.float32)
    x1, x2 = xr[..., :half], xr[..., half:]
    rot = jnp.concatenate([x1 * cos - x2 * sin, x2 * cos + x1 * sin], axis=-1).astype(x.dtype)
    return jnp.concatenate([rot, x[..., ROPE_DIM:]], axis=-1)


def moba_attention(q, k, v):
    B, H, S, Dh = q.shape
    nb = -(-S // MOBA_BLOCK)
    pad = nb * MOBA_BLOCK - S
    nqc = S // Q_CHUNK
    scale = Dh ** -0.5
    kp = jnp.pad(k, ((0, 0), (0, 0), (0, pad), (0, 0)))
    vp = jnp.pad(v, ((0, 0), (0, 0), (0, pad), (0, 0)))
    kb = kp.reshape(B, H, nb, MOBA_BLOCK, Dh)
    vb = vp.reshape(B, H, nb, MOBA_BLOCK, Dh)
    k_mean = jnp.mean(kb, axis=3, dtype=jnp.float32)
    gate = jnp.einsum('bhsd,bhnd->bhsn', q.astype(jnp.float32), k_mean)
    pos = jnp.arange(S)
    q_blk = pos // MOBA_BLOCK
    fully_past = jnp.arange(nb)[None, :] < q_blk[:, None]
    gate = jnp.where(fully_past, gate, -jnp.inf)
    k_sel = min(MOBA_TOPK, nb)
    top_val, top_idx = lax.top_k(gate, k_sel)
    top_ok = jnp.isfinite(top_val)
    own = jnp.broadcast_to(q_blk[None, None, :, None], (B, H, S, 1)).astype(top_idx.dtype)
    sel_idx = jnp.concatenate([top_idx, own], axis=-1)
    sel_ok = jnp.concatenate([top_ok, jnp.ones((B, H, S, 1), dtype=bool)], axis=-1)
    ns = k_sel + 1

    def to_chunks(t):
        tail = t.shape[3:]
        t = t.reshape((B, H, nqc, Q_CHUNK) + tail)
        t = jnp.moveaxis(t, 2, 1)
        return t.reshape((B * nqc, H, Q_CHUNK) + tail)

    q_c = to_chunks(q)
    idx_c = to_chunks(sel_idx)
    ok_c = to_chunks(sel_ok)
    b_ids = jnp.repeat(jnp.arange(B), nqc)
    qpos_c = jnp.tile(pos.reshape(nqc, Q_CHUNK), (B, 1))
    head_ix = jnp.arange(H)[:, None, None]
    key_off = jnp.arange(MOBA_BLOCK)

    def step(args):
        b, qi, idx, ok, qp = args
        k_g = kb[b][head_ix, idx]
        v_g = vb[b][head_ix, idx]
        s = jnp.einsum('hcd,hcnld->hcnl', qi, k_g, preferred_element_type=jnp.float32) * scale
        kpos = idx[..., None] * MOBA_BLOCK + key_off
        mask = ok[..., None] & (kpos <= qp[None, :, None, None])
        s = jnp.where(mask, s, NEG_INF)
        p = jax.nn.softmax(s.reshape(H, Q_CHUNK, ns * MOBA_BLOCK), axis=-1)
        p = p.reshape(H, Q_CHUNK, ns, MOBA_BLOCK).astype(v_g.dtype)
        o = jnp.einsum('hcnl,hcnld->hcd', p, v_g, preferred_element_type=jnp.float32)
        return o.astype(qi.dtype)

    out = lax.map(step, (b_ids, q_c, idx_c, ok_c, qpos_c))
    out = out.reshape(B, nqc, H, Q_CHUNK, Dh)
    return jnp.moveaxis(out, 1, 2).reshape(B, H, S, Dh)


def forgetting_attention(q, k, v, log_f):
    B, H, S, Dh = q.shape
    nqc = S // Q_CHUNK
    scale = Dh ** -0.5
    c = jnp.cumsum(log_f, axis=-1)
    pos = jnp.arange(S)
    q_c = jnp.moveaxis(q.reshape(B, H, nqc, Q_CHUNK, Dh), 2, 0)
    c_c = jnp.moveaxis(c.reshape(B, H, nqc, Q_CHUNK), 2, 0)
    qpos_c = pos.reshape(nqc, Q_CHUNK)

    def step(args):
        qi, ci, qp = args
        s = jnp.einsum('bhcd,bhsd->bhcs', qi, k, preferred_element_type=jnp.float32) * scale
        s = s + ci[..., None] - c[:, :, None, :]
        s = jnp.where(pos[None, :] <= qp[:, None], s, NEG_INF)
        p = jax.nn.softmax(s, axis=-1).astype(v.dtype)
        o = jnp.einsum('bhcs,bhsd->bhcd', p, v, preferred_element_type=jnp.float32)
        return o.astype(qi.dtype)

    out = lax.map(step, (q_c, c_c, qpos_c))
    return jnp.moveaxis(out, 0, 2).reshape(B, H, S, Dh)


def causal_depthwise_conv(t, w, bias):
    kern = w.reshape(CONV_WIDTH, 1, w.shape[-1]).astype(t.dtype)
    y = lax.conv_general_dilated(t, kern, window_strides=(1,), padding=[(CONV_WIDTH - 1, 0)],
                                 dimension_numbers=('NWC', 'WIO', 'NWC'),
                                 feature_group_count=t.shape[-1])
    return y + bias.astype(t.dtype)


def setup_inputs(seed: int = 0) -> dict:
    key = jax.random.key(seed)
    ks = jax.random.split(key, 18)
    f32 = jnp.float32

    def nrm(k, shape, scale):
        return jax.random.normal(k, shape, f32) * scale

    def gain(k, shape):
        return 1.0 + 0.02 * jax.random.normal(k, shape, f32)

    return {
        "x": nrm(ks[0], (BATCH, SEQ, D_MODEL), 1.0),
        "attn_norm_g": gain(ks[1], (DEPTH, D_MODEL)),
        "w_in": nrm(ks[2], (DEPTH, D_MODEL, IN_COLS), D_MODEL ** -0.5),
        "b_forget": nrm(ks[3], (DEPTH, N_FOX_HEADS), 0.1),
        "b_gate": nrm(ks[4], (DEPTH, N_BRANCH, D_MODEL), 0.1),
        "moba_q_norm_g": gain(ks[5], (DEPTH, HEAD_DIM)),
        "moba_k_norm_g": gain(ks[6], (DEPTH, HEAD_DIM)),
        "fox_q_norm_g": gain(ks[7], (DEPTH, HEAD_DIM)),
        "fox_k_norm_g": gain(ks[8], (DEPTH, HEAD_DIM)),
        "w_branch_moba": nrm(ks[9], (DEPTH, MOBA_WIDTH, D_MODEL), MOBA_WIDTH ** -0.5),
        "w_branch_fox": nrm(ks[10], (DEPTH, FOX_WIDTH, D_MODEL), FOX_WIDTH ** -0.5),
        "w_out": nrm(ks[11], (DEPTH, D_MODEL, D_MODEL), D_MODEL ** -0.5),
        "ffn_norm_g": gain(ks[12], (DEPTH, D_MODEL)),
        "w_ffn_up": nrm(ks[13], (DEPTH, D_MODEL, 2 * D_FF), D_MODEL ** -0.5),
        "ffn_conv_w": nrm(ks[14], (DEPTH, CONV_WIDTH, D_FF), CONV_WIDTH ** -0.5),
        "ffn_conv_b": nrm(ks[15], (DEPTH, D_FF), 0.02),
        "w_ffn_down": nrm(ks[16], (DEPTH, D_FF, D_MODEL), D_FF ** -0.5),
    }


def reference(x, attn_norm_g, w_in, b_forget, b_gate, moba_q_norm_g, moba_k_norm_g,
              fox_q_norm_g, fox_k_norm_g, w_branch_moba, w_branch_fox, w_out,
              ffn_norm_g, w_ffn_up, ffn_conv_w, ffn_conv_b, w_ffn_down):
    B, S, D = x.shape
    pos = jnp.arange(S)
    bounds = []
    acc = 0
    for w in IN_SPLITS[:-1]:
        acc += w
        bounds.append(acc)
    for l in range(DEPTH):
        h = rms_norm(x, attn_norm_g[l])
        proj = jnp.einsum('bsd,de->bse', h, w_in[l])
        qa, ka, va, qb, kb, vb, f_logit, g_logit = jnp.split(proj, bounds, axis=-1)
        qa = partial_rope(rms_norm(split_heads(qa, N_MOBA_HEADS), moba_q_norm_g[l]), pos)
        ka = partial_rope(rms_norm(split_heads(ka, N_MOBA_HEADS), moba_k_norm_g[l]), pos)
        va = split_heads(va, N_MOBA_HEADS)
        o_a = merge_heads(moba_attention(qa, ka, va))
        qb = rms_norm(split_heads(qb, N_FOX_HEADS), fox_q_norm_g[l])
        kb = rms_norm(split_heads(kb, N_FOX_HEADS), fox_k_norm_g[l])
        vb = split_heads(vb, N_FOX_HEADS)
        log_f = jax.nn.log_sigmoid(f_logit.astype(jnp.float32) + b_forget[l].astype(jnp.float32))
        log_f = log_f.transpose(0, 2, 1)
        o_b = merge_heads(forgetting_attention(qb, kb, vb, log_f))
        br_a = jnp.einsum('bse,ed->bsd', o_a, w_branch_moba[l])
        br_b = jnp.einsum('bse,ed->bsd', o_b, w_branch_fox[l])
        gates = jax.nn.sigmoid(g_logit.astype(jnp.float32).reshape(B, S, N_BRANCH, D)
                               + b_gate[l].astype(jnp.float32)).astype(x.dtype)
        merged = gates[:, :, 0] * br_a + gates[:, :, 1] * br_b
        x = x + jnp.einsum('bsd,de->bse', merged, w_out[l])
        h2 = rms_norm(x, ffn_norm_g[l])
        up = jnp.einsum('bsd,df->bsf', h2, w_ffn_up[l])
        u, g = jnp.split(up, 2, axis=-1)
        g = causal_depthwise_conv(g, ffn_conv_w[l], ffn_conv_b[l])
        x = x + jnp.einsum('bsf,fd->bsd', jax.nn.silu(g) * u, w_ffn_down[l])
    return x
```

```python
import functools
import math

import jax
import jax.numpy as jnp
from jax import lax
from jax.experimental import pallas as pl
from jax.experimental.pallas import tpu as pltpu

HEAD_DIM = 64
N_BRANCH_HEADS = 8
N_HEADS = 2 * N_BRANCH_HEADS
BRANCH_WIDTH = N_BRANCH_HEADS * HEAD_DIM
MOBA_BLOCK = 256
MOBA_TOPK = 3
ROPE_THETA = 500000.0
ROPE_DIM = HEAD_DIM // 4
ROPE_HALF = ROPE_DIM // 2
CONV_WIDTH = 3
NORM_EPS = 1e-6
NEG_INF = -1e30

LANES = 128
AUG_ROWS = 16
QK_ROWS = 128
V_ROWS = HEAD_DIM + 16
VMEM_LIMIT = 56 * 1024 * 1024

F32 = jnp.float32
BF16 = jnp.bfloat16


def _split3(x):
    a = x.astype(BF16)
    r = x - a.astype(F32)
    b = r.astype(BF16)
    c = (r - b.astype(F32)).astype(BF16)
    return a, b, c


def _const_spec(shape):
    n = len(shape)
    return pl.BlockSpec(shape, lambda *_: (0,) * n, pipeline_mode=pl.Buffered(1))


def _proj_kernel(x_ref, g_ref, wt_ref, gain_ref, bf_ref, cos_ref, sin_ref,
                 q_ref, k_ref, v_ref, kmean_scr, carry_scr, *, tm):
    i = pl.program_id(1)

    @pl.when(i == 0)
    def _():
        kmean_scr[...] = jnp.zeros_like(kmean_scr)
        carry_scr[...] = jnp.zeros_like(carry_scr)

    x = x_ref[0]
    ms = jnp.mean(x * x, axis=-1, keepdims=True)
    h = (x * lax.rsqrt(ms + NORM_EPS) * g_ref[...]).astype(BF16)
    pt = lax.dot_general(wt_ref[...], h, (((1,), (1,)), ((), ())),
                         preferred_element_type=F32)

    cos = cos_ref[...]
    sin = sin_ref[...]
    row16 = lax.broadcasted_iota(jnp.int32, (AUG_ROWS, tm), 0)
    zeros_pad = jnp.zeros((QK_ROWS - HEAD_DIM - AUG_ROWS, tm), F32)
    ones_row = jnp.where(row16 == 0, 1.0, 0.0).astype(BF16)

    def tile_lanes(a):
        return jnp.concatenate([a] * (tm // LANES), axis=1)

    def head_norm(t, gain):
        m = jnp.mean(t * t, axis=0, keepdims=True)
        return t * lax.rsqrt(m + NORM_EPS) * tile_lanes(gain)

    def rope(t):
        x1 = t[0:ROPE_HALF]
        x2 = t[ROPE_HALF:ROPE_DIM]
        return jnp.concatenate(
            [x1 * cos - x2 * sin, x2 * cos + x1 * sin, t[ROPE_DIM:]], axis=0)

    def emit_k(hd, kt, aug):
        kaug_t = jnp.concatenate([kt, aug, zeros_pad], axis=0)
        k_ref[0, hd] = kaug_t.T.astype(BF16)

    def emit_q(hd, qt, aug):
        q_ref[0, hd, 0:HEAD_DIM, :] = qt.astype(BF16)
        q_ref[0, hd, HEAD_DIM:HEAD_DIM + AUG_ROWS, :] = aug.astype(BF16)
        q_ref[0, hd, HEAD_DIM + AUG_ROWS:, :] = zeros_pad.astype(BF16)

    def emit_v(hd, vt):
        v_ref[0, hd, 0, 0:HEAD_DIM, :] = vt.astype(BF16)
        v_ref[0, hd, 0, HEAD_DIM:, :] = ones_row

    def rows(base, hd):
        r0 = base + hd * HEAD_DIM
        return pt[r0:r0 + HEAD_DIM, :]

    gq_a = gain_ref[0 * HEAD_DIM:1 * HEAD_DIM, :]
    gk_a = gain_ref[1 * HEAD_DIM:2 * HEAD_DIM, :]
    gq_b = gain_ref[2 * HEAD_DIM:3 * HEAD_DIM, :]
    gk_b = gain_ref[3 * HEAD_DIM:4 * HEAD_DIM, :]
    own_onehot = jnp.where(row16 == i, 1.0, 0.0)
    row16f = row16.astype(F32)
    neg_inf = jnp.float32(-jnp.inf)
    head_lanes = lax.broadcasted_iota(jnp.int32, (1, QK_ROWS), 1) < HEAD_DIM
    zeros_head = jnp.zeros((QK_ROWS - HEAD_DIM, tm), F32)
    for hd in range(N_BRANCH_HEADS):
        qt = rope(head_norm(rows(0, hd), gq_a))
        kt = rope(head_norm(rows(BRANCH_WIDTH, hd), gk_a))
        vt = rows(2 * BRANCH_WIDTH, hd)
        kaug = jnp.concatenate([kt, own_onehot, zeros_pad], axis=0).T
        k_ref[0, hd] = kaug.astype(BF16)
        kmean = jnp.sum(kaug, axis=0, keepdims=True) * (1.0 / tm)
        kmean = jnp.where(head_lanes, kmean, 0.0)
        table = jnp.where(row16[:, 0:QK_ROWS] == i, kmean, kmean_scr[hd])
        kmean_scr[hd] = table
        t_hi, t_lo, _ = _split3(table)
        q_hi, q_lo, _ = _split3(jnp.concatenate([qt, zeros_head], axis=0))
        gate = (jnp.dot(t_hi, q_hi, preferred_element_type=F32)
                + jnp.dot(t_hi, q_lo, preferred_element_type=F32)
                + jnp.dot(t_lo, q_hi, preferred_element_type=F32))
        gate = jnp.where(row16 < i, gate, neg_inf)
        sel = row16 == i
        for _ in range(MOBA_TOPK):
            mx = jnp.max(gate, axis=0, keepdims=True)
            is_mx = jnp.logical_and(gate == mx, mx > neg_inf)
            first = jnp.min(jnp.where(is_mx, row16f, float(AUG_ROWS)), axis=0, keepdims=True)
            pick = row16f == first
            sel = jnp.logical_or(sel, pick)
            gate = jnp.where(pick, neg_inf, gate)
        bias = jnp.where(sel, 0.0, NEG_INF)
        emit_q(hd, qt, bias)
        emit_v(hd, vt)

    z = pt[6 * BRANCH_WIDTH:6 * BRANCH_WIDTH + AUG_ROWS, :] + tile_lanes(bf_ref[...])
    logf = jnp.minimum(z, 0.0) - jnp.log1p(jnp.exp(-jnp.abs(z)))
    logf = jnp.where(row16 < N_BRANCH_HEADS, logf, 0.0)
    r_i = lax.broadcasted_iota(jnp.int32, (tm, tm), 0)
    c_i = lax.broadcasted_iota(jnp.int32, (tm, tm), 1)
    tri = jnp.where(r_i <= c_i, 1.0, 0.0).astype(BF16)
    ones_mat = jnp.ones((tm, LANES), BF16)
    parts = _split3(logf)
    c_loc = sum(jnp.dot(p, tri, preferred_element_type=F32) for p in parts)
    total = sum(jnp.dot(p, ones_mat, preferred_element_type=F32) for p in parts)
    carry = carry_scr[...]
    c = c_loc + tile_lanes(carry)
    carry_scr[...] = carry + total
    c1, c2, c3 = (p.astype(F32) for p in _split3(c))

    def aug_rows(a, b, d, first_ones):
        o = 3 if first_ones else 0
        abd = jnp.where(row16 == o, a, jnp.where(row16 == o + 1, b, d))
        in_abd = jnp.logical_and(row16 >= o, row16 < o + 3)
        return jnp.where(in_abd, abd, jnp.where(row16 < 6, 1.0, 0.0))

    for hd in range(N_BRANCH_HEADS):
        qt = head_norm(rows(3 * BRANCH_WIDTH, hd), gq_b)
        kt = head_norm(rows(4 * BRANCH_WIDTH, hd), gk_b)
        vt = rows(5 * BRANCH_WIDTH, hd)
        a = jnp.broadcast_to(c1[hd:hd + 1], (AUG_ROWS, tm))
        b = jnp.broadcast_to(c2[hd:hd + 1], (AUG_ROWS, tm))
        d = jnp.broadcast_to(c3[hd:hd + 1], (AUG_ROWS, tm))
        emit_q(N_BRANCH_HEADS + hd, qt, aug_rows(a, b, d, first_ones=False))
        emit_k(N_BRANCH_HEADS + hd, kt, aug_rows(-a, -b, -d, first_ones=True))
        emit_v(N_BRANCH_HEADS + hd, vt)


def _proj(x, attn_g, wt, gains, bf, cos_t, sin_t, *, tm):
    B, S, D = x.shape
    ns = S // tm
    rows = wt.shape[0]
    kern = functools.partial(_proj_kernel, tm=tm)
    return pl.pallas_call(
        kern,
        grid=(B, ns),
        in_specs=[
            pl.BlockSpec((1, tm, D), lambda b, i: (b, i, 0)),
            _const_spec((1, D)),
            _const_spec((rows, D)),
            _const_spec((4 * HEAD_DIM, LANES)),
            _const_spec((AUG_ROWS, LANES)),
            pl.BlockSpec((ROPE_HALF, tm), lambda b, i: (0, i)),
            pl.BlockSpec((ROPE_HALF, tm), lambda b, i: (0, i)),
        ],
        out_specs=[
            pl.BlockSpec((1, N_HEADS, QK_ROWS, tm), lambda b, i: (b, 0, 0, i)),
            pl.BlockSpec((1, N_HEADS, tm, QK_ROWS), lambda b, i: (b, 0, i, 0)),
            pl.BlockSpec((1, N_HEADS, 1, V_ROWS, tm), lambda b, i: (b, 0, i, 0, 0)),
        ],
        out_shape=[
            jax.ShapeDtypeStruct((B, N_HEADS, QK_ROWS, S), BF16),
            jax.ShapeDtypeStruct((B, N_HEADS, S, QK_ROWS), BF16),
            jax.ShapeDtypeStruct((B, N_HEADS, ns, V_ROWS, tm), BF16),
        ],
        scratch_shapes=[
            pltpu.VMEM((N_BRANCH_HEADS, AUG_ROWS, QK_ROWS), F32),
            pltpu.VMEM((AUG_ROWS, LANES), F32),
        ],
        compiler_params=pltpu.CompilerParams(
            dimension_semantics=("arbitrary", "arbitrary"),
            vmem_limit_bytes=VMEM_LIMIT),
        name="proj",
    )(x, attn_g, wt, gains, bf, cos_t, sin_t)


def _attn_kernel(q_ref, k_ref, v_ref, o_ref, s_scr, m_scr, acc_scr, *, hg, t):
    qi = pl.program_id(2)

    def scores(hd, kj):
        k = k_ref[0, hd, pl.ds(pl.multiple_of(kj * t, t), t), :]
        return jnp.dot(k, q_ref[0, hd], preferred_element_type=F32)

    def accumulate(hd, s, kj):
        m_old = m_scr[hd]
        m_new = jnp.maximum(m_old, jnp.max(s, axis=0, keepdims=True))
        alpha = jnp.exp(m_old - m_new)
        p = jnp.exp(s - m_new).astype(BF16)
        pv = jnp.dot(v_ref[0, hd, kj], p, preferred_element_type=F32)
        acc_scr[hd] = alpha * acc_scr[hd] + pv
        m_scr[hd] = m_new

    for hd in range(hg):
        m_scr[hd] = jnp.full((1, t), -jnp.inf, F32)
        acc_scr[hd] = jnp.zeros((V_ROWS, t), F32)
        s_scr[0, hd] = scores(hd, 0)

    def body(kj, carry):
        cur = kj % 2
        for hd in range(hg):
            s_scr[1 - cur, hd] = scores(hd, kj + 1)
            accumulate(hd, s_scr[cur, hd], kj)
        return carry

    lax.fori_loop(0, qi, body, 0)

    key_i = lax.broadcasted_iota(jnp.int32, (t, t), 0)
    qry_i = lax.broadcasted_iota(jnp.int32, (t, t), 1)
    causal = key_i <= qry_i
    last = qi % 2
    for hd in range(hg):
        s = jnp.where(causal, s_scr[last, hd], -jnp.inf)
        accumulate(hd, s, qi)
        acc = acc_scr[hd]
        o = acc[0:HEAD_DIM] / acc[HEAD_DIM:HEAD_DIM + 1]
        o_ref[0, hd * HEAD_DIM:(hd + 1) * HEAD_DIM, :] = o.astype(BF16)


def _attn(q, k, v, *, hg, t):
    B, H, _, S = q.shape
    nq = S // t
    kern = functools.partial(_attn_kernel, hg=hg, t=t)
    return pl.pallas_call(
        kern,
        grid=(B, H // hg, nq),
        in_specs=[
            pl.BlockSpec((1, hg, QK_ROWS, t), lambda b, g, i: (b, g, 0, i)),
            pl.BlockSpec((1, hg, S, QK_ROWS), lambda b, g, i: (b, g, 0, 0)),
            pl.BlockSpec((1, hg, nq, V_ROWS, t), lambda b, g, i: (b, g, 0, 0, 0)),
        ],
        out_specs=pl.BlockSpec((1, hg * HEAD_DIM, t), lambda b, g, i: (b, g, i)),
        out_shape=jax.ShapeDtypeStruct((B, H * HEAD_DIM, S), BF16),
        scratch_shapes=[
            pltpu.VMEM((2, hg, t, t), F32),
            pltpu.VMEM((hg, 1, t), F32),
            pltpu.VMEM((hg, V_ROWS, t), F32),
        ],
        compiler_params=pltpu.CompilerParams(
            dimension_semantics=("arbitrary", "arbitrary", "arbitrary"),
            vmem_limit_bytes=VMEM_LIMIT),
        name="attn",
    )(q, k, v)


def _merge_kernel(x_ref, ot_ref, g_ref, wg_ref, bg_ref, wa_ref, wb_ref, wo_ref, y_ref):
    x = x_ref[0]
    d = x.shape[-1]
    ms = jnp.mean(x * x, axis=-1, keepdims=True)
    h = (x * lax.rsqrt(ms + NORM_EPS) * g_ref[...]).astype(BF16)
    glog = jnp.dot(h, wg_ref[...], preferred_element_type=F32) + bg_ref[...]
    gates = jax.nn.sigmoid(glog)
    dn = (((0,), (0,)), ((), ()))
    br_a = lax.dot_general(ot_ref[0, 0:BRANCH_WIDTH, :], wa_ref[...], dn,
                           preferred_element_type=F32)
    br_b = lax.dot_general(ot_ref[0, BRANCH_WIDTH:, :], wb_ref[...], dn,
                           preferred_element_type=F32)
    merged = (gates[:, :d] * br_a + gates[:, d:] * br_b).astype(BF16)
    y_ref[0] = x + jnp.dot(merged, wo_ref[...], preferred_element_type=F32)


def _merge(x, ot, attn_g, wg, bg, wa, wb, wo, *, tm):
    B, S, D = x.shape
    return pl.pallas_call(
        _merge_kernel,
        grid=(B, S // tm),
        in_specs=[
            pl.BlockSpec((1, tm, D), lambda b, i: (b, i, 0)),
            pl.BlockSpec((1, 2 * BRANCH_WIDTH, tm), lambda b, i: (b, 0, i)),
            _const_spec((1, D)),
            _const_spec((D, 2 * D)),
            _const_spec((1, 2 * D)),
            _const_spec((BRANCH_WIDTH, D)),
            _const_spec((BRANCH_WIDTH, D)),
            _const_spec((D, D)),
        ],
        out_specs=pl.BlockSpec((1, tm, D), lambda b, i: (b, i, 0)),
        out_shape=jax.ShapeDtypeStruct((B, S, D), F32),
        compiler_params=pltpu.CompilerParams(
            dimension_semantics=("arbitrary", "arbitrary"),
            vmem_limit_bytes=VMEM_LIMIT),
        name="merge",
    )(x, ot, attn_g, wg, bg, wa, wb, wo)


def _ffn_kernel(x_ref, g_ref, wup_ref, cw_ref, cb_ref, wdn_ref, y_ref, gs_scr, *, tm, dff):
    i = pl.program_id(1)

    @pl.when(i == 0)
    def _():
        gs_scr[0:8, :] = jnp.zeros((8, dff), F32)

    x = x_ref[0]
    ms = jnp.mean(x * x, axis=-1, keepdims=True)
    h = (x * lax.rsqrt(ms + NORM_EPS) * g_ref[...]).astype(BF16)
    up = jnp.dot(h, wup_ref[...], preferred_element_type=F32)
    u = up[:, :dff]
    g = up[:, dff:]
    gs_scr[8:8 + tm, :] = g
    g_m1 = gs_scr[7:7 + tm, :]
    g_m2 = gs_scr[6:6 + tm, :]
    gc = (cw_ref[2:3, :] * g + cw_ref[1:2, :] * g_m1 + cw_ref[0:1, :] * g_m2
          + cb_ref[...])
    gs_scr[0:8, :] = g[tm - 8:tm, :]
    act = (gc * jax.nn.sigmoid(gc) * u).astype(BF16)
    y_ref[0] = x + jnp.dot(act, wdn_ref[...], preferred_element_type=F32)


def _ffn(x, ffn_g, wup, cw, cb, wdn, *, tm):
    B, S, D = x.shape
    dff = wdn.shape[0]
    kern = functools.partial(_ffn_kernel, tm=tm, dff=dff)
    return pl.pallas_call(
        kern,
        grid=(B, S // tm),
        in_specs=[
            pl.BlockSpec((1, tm, D), lambda b, i: (b, i, 0)),
            _const_spec((1, D)),
            _const_spec((D, 2 * dff)),
            _const_spec((CONV_WIDTH, dff)),
            _const_spec((1, dff)),
            _const_spec((dff, D)),
        ],
        out_specs=pl.BlockSpec((1, tm, D), lambda b, i: (b, i, 0)),
        out_shape=jax.ShapeDtypeStruct((B, S, D), F32),
        scratch_shapes=[pltpu.VMEM((tm + 8, dff), F32)],
        compiler_params=pltpu.CompilerParams(
            dimension_semantics=("arbitrary", "arbitrary"),
            vmem_limit_bytes=VMEM_LIMIT),
        name="ffn",
    )(x, ffn_g, wup, cw, cb, wdn)


def _layer(x, attn_norm_g, w_in, b_forget, b_gate, moba_q_norm_g, moba_k_norm_g,
           fox_q_norm_g, fox_k_norm_g, w_branch_moba, w_branch_fox, w_out,
           ffn_norm_g, w_ffn_up, ffn_conv_w, ffn_conv_b, w_ffn_down):
    B, S, D = x.shape
    nqkv = 6 * BRANCH_WIDTH
    scale = HEAD_DIM ** -0.5

    w_f = jnp.pad(w_in[:, nqkv:nqkv + N_BRANCH_HEADS], ((0, 0), (0, AUG_ROWS - N_BRANCH_HEADS)))
    wt = jnp.concatenate([w_in[:, :nqkv], w_f], axis=1).T.astype(BF16)
    wg = w_in[:, nqkv + N_BRANCH_HEADS:].astype(BF16)
    gains = jnp.concatenate([moba_q_norm_g * scale, moba_k_norm_g,
                             fox_q_norm_g * scale, fox_k_norm_g])
    gains = jnp.broadcast_to(gains[:, None], (4 * HEAD_DIM, LANES)).astype(F32)
    bf = jnp.pad(b_forget.astype(F32), (0, AUG_ROWS - N_BRANCH_HEADS))
    bf = jnp.broadcast_to(bf[:, None], (AUG_ROWS, LANES))
    inv_freq = jnp.power(ROPE_THETA, -jnp.arange(ROPE_HALF, dtype=F32) * 2.0 / ROPE_DIM)
    ang = jnp.arange(S).astype(F32)[:, None] * inv_freq[None, :]
    cos_t, sin_t = jnp.cos(ang).T, jnp.sin(ang).T

    q, k, v = _proj(x, attn_norm_g.reshape(1, D), wt, gains, bf, cos_t, sin_t, tm=MOBA_BLOCK)
    ot = _attn(q, k, v, hg=4, t=MOBA_BLOCK)
    x1 = _merge(x, ot, attn_norm_g.reshape(1, D), wg, b_gate.reshape(1, 2 * D).astype(F32),
                w_branch_moba.astype(BF16), w_branch_fox.astype(BF16), w_out.astype(BF16),
                tm=512)
    return _ffn(x1, ffn_norm_g.reshape(1, D), w_ffn_up.astype(BF16), ffn_conv_w.astype(F32),
                ffn_conv_b.reshape(1, -1).astype(F32), w_ffn_down.astype(BF16), tm=256)


def kernel(x, attn_norm_g, w_in, b_forget, b_gate, moba_q_norm_g, moba_k_norm_g,
           fox_q_norm_g, fox_k_norm_g, w_branch_moba, w_branch_fox, w_out,
           ffn_norm_g, w_ffn_up, ffn_conv_w, ffn_conv_b, w_ffn_down):
    for l in range(attn_norm_g.shape[0]):
        x = _layer(x, attn_norm_g[l], w_in[l], b_forget[l], b_gate[l], moba_q_norm_g[l],
                   moba_k_norm_g[l], fox_q_norm_g[l], fox_k_norm_g[l], w_branch_moba[l],
                   w_branch_fox[l], w_out[l], ffn_norm_g[l], w_ffn_up[l], ffn_conv_w[l],
                   ffn_conv_b[l], w_ffn_down[l])
    return x
```

```python
import functools
import math

import jax
import jax.numpy as jnp
from jax import lax
from jax.experimental import pallas as pl
from jax.experimental.pallas import tpu as pltpu

HEAD_DIM = 64
N_BRANCH_HEADS = 8
N_HEADS = 2 * N_BRANCH_HEADS
BRANCH_WIDTH = N_BRANCH_HEADS * HEAD_DIM
MOBA_BLOCK = 256
MOBA_TOPK = 3
ROPE_THETA = 500000.0
ROPE_DIM = HEAD_DIM // 4
ROPE_HALF = ROPE_DIM // 2
CONV_WIDTH = 3
NORM_EPS = 1e-6
NEG_INF = -1e30

LANES = 128
AUG_ROWS = 16
QK_ROWS = 128
V_ROWS = HEAD_DIM + 16
VMEM_LIMIT = 56 * 1024 * 1024

F32 = jnp.float32
BF16 = jnp.bfloat16


def _split3(x):
    a = x.astype(BF16)
    r = x - a.astype(F32)
    b = r.astype(BF16)
    c = (r - b.astype(F32)).astype(BF16)
    return a, b, c


def _const_spec(shape):
    n = len(shape)
    return pl.BlockSpec(shape, lambda *_: (0,) * n, pipeline_mode=pl.Buffered(1))


def _proj_kernel(x_ref, g_ref, wt_ref, gain_ref, bf_ref, cos_ref, sin_ref,
                 q_ref, k_ref, v_ref, kmean_scr, carry_scr, *, tm):
    i = pl.program_id(1)

    @pl.when(i == 0)
    def _():
        kmean_scr[...] = jnp.zeros_like(kmean_scr)
        carry_scr[...] = jnp.zeros_like(carry_scr)

    x = x_ref[0]
    ms = jnp.mean(x * x, axis=-1, keepdims=True)
    h = (x * lax.rsqrt(ms + NORM_EPS) * g_ref[...]).astype(BF16)
    pt = lax.dot_general(wt_ref[...], h, (((1,), (1,)), ((), ())),
                         preferred_element_type=F32)

    cos = cos_ref[...]
    sin = sin_ref[...]
    row16 = lax.broadcasted_iota(jnp.int32, (AUG_ROWS, tm), 0)
    zeros_pad = jnp.zeros((QK_ROWS - HEAD_DIM - AUG_ROWS, tm), F32)
    ones_row = jnp.where(row16 == 0, 1.0, 0.0).astype(BF16)

    def tile_lanes(a):
        return jnp.concatenate([a] * (tm // LANES), axis=1)

    def head_norm(t, gain):
        m = jnp.mean(t * t, axis=0, keepdims=True)
        return t * lax.rsqrt(m + NORM_EPS) * tile_lanes(gain)

    def rope(t):
        x1 = t[0:ROPE_HALF]
        x2 = t[ROPE_HALF:ROPE_DIM]
        return jnp.concatenate(
            [x1 * cos - x2 * sin, x2 * cos + x1 * sin, t[ROPE_DIM:]], axis=0)

    def emit_k(hd, kt, aug):
        kaug_t = jnp.concatenate([kt, aug, zeros_pad], axis=0)
        k_ref[0, hd] = kaug_t.T.astype(BF16)

    def emit_q(hd, qt, aug):
        q_ref[0, hd, 0:HEAD_DIM, :] = qt.astype(BF16)
        q_ref[0, hd, HEAD_DIM:HEAD_DIM + AUG_ROWS, :] = aug.astype(BF16)
        q_ref[0, hd, HEAD_DIM + AUG_ROWS:, :] = zeros_pad.astype(BF16)

    def emit_v(hd, vt):
        v_ref[0, hd, 0, 0:HEAD_DIM, :] = vt.astype(BF16)
        v_ref[0, hd, 0, HEAD_DIM:, :] = ones_row

    def rows(base, hd):
        r0 = base + hd * HEAD_DIM
        return pt[r0:r0 + HEAD_DIM, :]

    gq_a = gain_ref[0 * HEAD_DIM:1 * HEAD_DIM, :]
    gk_a = gain_ref[1 * HEAD_DIM:2 * HEAD_DIM, :]
    gq_b = gain_ref[2 * HEAD_DIM:3 * HEAD_DIM, :]
    gk_b = gain_ref[3 * HEAD_DIM:4 * HEAD_DIM, :]
    own_onehot = jnp.where(row16 == i, 1.0, 0.0)
    row16f = row16.astype(F32)
    neg_inf = jnp.float32(-jnp.inf)
    head_lanes = lax.broadcasted_iota(jnp.int32, (1, QK_ROWS), 1) < HEAD_DIM
    zeros_head = jnp.zeros((QK_ROWS - HEAD_DIM, tm), F32)
    for hd in range(N_BRANCH_HEADS):
        qt = rope(head_norm(rows(0, hd), gq_a))
        kt = rope(head_norm(rows(BRANCH_WIDTH, hd), gk_a))
        vt = rows(2 * BRANCH_WIDTH, hd)
        kaug = jnp.concatenate([kt, own_onehot, zeros_pad], axis=0).T
        k_ref[0, hd] = kaug.astype(BF16)
        kmean = jnp.sum(kaug, axis=0, keepdims=True) * (1.0 / tm)
        kmean = jnp.where(head_lanes, kmean, 0.0)
        table = jnp.where(row16[:, 0:QK_ROWS] == i, kmean, kmean_scr[hd])
        kmean_scr[hd] = table
        t_hi, t_lo, _ = _split3(table)
        q_hi, q_lo, _ = _split3(jnp.concatenate([qt, zeros_head], axis=0))
        gate = (jnp.dot(t_hi, q_hi, preferred_element_type=F32)
                + jnp.dot(t_hi, q_lo, preferred_element_type=F32)
                + jnp.dot(t_lo, q_hi, preferred_element_type=F32))
        gate = jnp.where(row16 < i, gate, neg_inf)
        sel = row16 == i
        for _ in range(MOBA_TOPK):
            mx = jnp.max(gate, axis=0, keepdims=True)
            is_mx = jnp.logical_and(gate == mx, mx > neg_inf)
            first = jnp.min(jnp.where(is_mx, row16f, float(AUG_ROWS)), axis=0, keepdims=True)
            pick = row16f == first
            sel = jnp.logical_or(sel, pick)
            gate = jnp.where(pick, neg_inf, gate)
        bias = jnp.where(sel, 0.0, NEG_INF)
        emit_q(hd, qt, bias)
        emit_v(hd, vt)

    z = pt[6 * BRANCH_WIDTH:6 * BRANCH_WIDTH + AUG_ROWS, :] + tile_lanes(bf_ref[...])
    logf = jnp.minimum(z, 0.0) - jnp.log1p(jnp.exp(-jnp.abs(z)))
    logf = jnp.where(row16 < N_BRANCH_HEADS, logf, 0.0)
    r_i = lax.broadcasted_iota(jnp.int32, (tm, tm), 0)
    c_i = lax.broadcasted_iota(jnp.int32, (tm, tm), 1)
    tri = jnp.where(r_i <= c_i, 1.0, 0.0).astype(BF16)
    ones_mat = jnp.ones((tm, LANES), BF16)
    parts = _split3(logf)
    c_loc = sum(jnp.dot(p, tri, preferred_element_type=F32) for p in parts)
    total = sum(jnp.dot(p, ones_mat, preferred_element_type=F32) for p in parts)
    carry = carry_scr[...]
    c = c_loc + tile_lanes(carry)
    carry_scr[...] = carry + total
    c1, c2, c3 = (p.astype(F32) for p in _split3(c))

    def aug_rows(a, b, d, first_ones):
        o = 3 if first_ones else 0
        abd = jnp.where(row16 == o, a, jnp.where(row16 == o + 1, b, d))
        in_abd = jnp.logical_and(row16 >= o, row16 < o + 3)
        return jnp.where(in_abd, abd, jnp.where(row16 < 6, 1.0, 0.0))

    for hd in range(N_BRANCH_HEADS):
        qt = head_norm(rows(3 * BRANCH_WIDTH, hd), gq_b)
        kt = head_norm(rows(4 * BRANCH_WIDTH, hd), gk_b)
        vt = rows(5 * BRANCH_WIDTH, hd)
        a = jnp.broadcast_to(c1[hd:hd + 1], (AUG_ROWS, tm))
        b = jnp.broadcast_to(c2[hd:hd + 1], (AUG_ROWS, tm))
        d = jnp.broadcast_to(c3[hd:hd + 1], (AUG_ROWS, tm))
        emit_q(N_BRANCH_HEADS + hd, qt, aug_rows(a, b, d, first_ones=False))
        emit_k(N_BRANCH_HEADS + hd, kt, aug_rows(-a, -b, -d, first_ones=True))
        emit_v(N_BRANCH_HEADS + hd, vt)


def _proj(x, attn_g, wt, gains, bf, cos_t, sin_t, *, tm):
    B, S, D = x.shape
    ns = S // tm
    rows = wt.shape[0]
    kern = functools.partial(_proj_kernel, tm=tm)
    return pl.pallas_call(
        kern,
        grid=(B, ns),
        in_specs=[
            pl.BlockSpec((1, tm, D), lambda b, i: (b, i, 0)),
            _const_spec((1, D)),
            _const_spec((rows, D)),
            _const_spec((4 * HEAD_DIM, LANES)),
            _const_spec((AUG_ROWS, LANES)),
            pl.BlockSpec((ROPE_HALF, tm), lambda b, i: (0, i)),
            pl.BlockSpec((ROPE_HALF, tm), lambda b, i: (0, i)),
        ],
        out_specs=[
            pl.BlockSpec((1, N_HEADS, QK_ROWS, tm), lambda b, i: (b, 0, 0, i)),
            pl.BlockSpec((1, N_HEADS, tm, QK_ROWS), lambda b, i: (b, 0, i, 0)),
            pl.BlockSpec((1, N_HEADS, 1, V_ROWS, tm), lambda b, i: (b, 0, i, 0, 0)),
        ],
        out_shape=[
            jax.ShapeDtypeStruct((B, N_HEADS, QK_ROWS, S), BF16),
            jax.ShapeDtypeStruct((B, N_HEADS, S, QK_ROWS), BF16),
            jax.ShapeDtypeStruct((B, N_HEADS, ns, V_ROWS, tm), BF16),
        ],
        scratch_shapes=[
            pltpu.VMEM((N_BRANCH_HEADS, AUG_ROWS, QK_ROWS), F32),
            pltpu.VMEM((AUG_ROWS, LANES), F32),
        ],
        compiler_params=pltpu.CompilerParams(
            dimension_semantics=("arbitrary", "arbitrary"),
            vmem_limit_bytes=VMEM_LIMIT),
        name="proj",
    )(x, attn_g, wt, gains, bf, cos_t, sin_t)


def _attn_kernel(q_ref, k_ref, v_ref, o_ref, m_scr, acc_scr, *, hg, t):
    qi = pl.program_id(2)

    def scores(hd, kj):
        k = k_ref[0, hd, pl.ds(pl.multiple_of(kj * t, t), t), :]
        return jnp.dot(k, q_ref[0, hd], preferred_element_type=F32)

    def accumulate(hd, s, kj):
        m_old = m_scr[hd]
        m_new = jnp.maximum(m_old, jnp.max(s, axis=0, keepdims=True))
        alpha = jnp.exp(m_old - m_new)
        p = jnp.exp(s - m_new).astype(BF16)
        pv = jnp.dot(v_ref[0, hd, kj], p, preferred_element_type=F32)
        acc_scr[hd] = alpha * acc_scr[hd] + pv
        m_scr[hd] = m_new

    def tiles(kj, mask):
        ss = [scores(hd, kj) for hd in range(hg)]
        for hd in range(hg):
            s = ss[hd] if mask is None else jnp.where(mask, ss[hd], -jnp.inf)
            accumulate(hd, s, kj)

    for hd in range(hg):
        m_scr[hd] = jnp.full((1, t), -jnp.inf, F32)
        acc_scr[hd] = jnp.zeros((V_ROWS, t), F32)

    def body(kj, carry):
        tiles(kj, None)
        return carry

    lax.fori_loop(0, qi, body, 0)

    key_i = lax.broadcasted_iota(jnp.int32, (t, t), 0)
    qry_i = lax.broadcasted_iota(jnp.int32, (t, t), 1)
    tiles(qi, key_i <= qry_i)
    for hd in range(hg):
        acc = acc_scr[hd]
        o = acc[0:HEAD_DIM] / acc[HEAD_DIM:HEAD_DIM + 1]
        o_ref[0, hd * HEAD_DIM:(hd + 1) * HEAD_DIM, :] = o.astype(BF16)


def _attn(q, k, v, *, hg, t):
    B, H, _, S = q.shape
    nq = S // t
    kern = functools.partial(_attn_kernel, hg=hg, t=t)
    return pl.pallas_call(
        kern,
        grid=(B, H // hg, nq),
        in_specs=[
            pl.BlockSpec((1, hg, QK_ROWS, t), lambda b, g, i: (b, g, 0, i)),
            pl.BlockSpec((1, hg, S, QK_ROWS), lambda b, g, i: (b, g, 0, 0)),
            pl.BlockSpec((1, hg, nq, V_ROWS, t), lambda b, g, i: (b, g, 0, 0, 0)),
        ],
        out_specs=pl.BlockSpec((1, hg * HEAD_DIM, t), lambda b, g, i: (b, g, i)),
        out_shape=jax.ShapeDtypeStruct((B, H * HEAD_DIM, S), BF16),
        scratch_shapes=[
            pltpu.VMEM((hg, 1, t), F32),
            pltpu.VMEM((hg, V_ROWS, t), F32),
        ],
        compiler_params=pltpu.CompilerParams(
            dimension_semantics=("arbitrary", "arbitrary", "arbitrary"),
            vmem_limit_bytes=VMEM_LIMIT),
        name="attn",
    )(q, k, v)


def _merge_kernel(x_ref, ot_ref, g_ref, wg_ref, bg_ref, wa_ref, wb_ref, wo_ref, y_ref):
    x = x_ref[0]
    d = x.shape[-1]
    ms = jnp.mean(x * x, axis=-1, keepdims=True)
    h = (x * lax.rsqrt(ms + NORM_EPS) * g_ref[...]).astype(BF16)
    glog = jnp.dot(h, wg_ref[...], preferred_element_type=F32) + bg_ref[...]
    gates = jax.nn.sigmoid(glog)
    dn = (((0,), (0,)), ((), ()))
    br_a = lax.dot_general(ot_ref[0, 0:BRANCH_WIDTH, :], wa_ref[...], dn,
                           preferred_element_type=F32)
    br_b = lax.dot_general(ot_ref[0, BRANCH_WIDTH:, :], wb_ref[...], dn,
                           preferred_element_type=F32)
    merged = (gates[:, :d] * br_a + gates[:, d:] * br_b).astype(BF16)
    y_ref[0] = x + jnp.dot(merged, wo_ref[...], preferred_element_type=F32)


def _merge(x, ot, attn_g, wg, bg, wa, wb, wo, *, tm):
    B, S, D = x.shape
    return pl.pallas_call(
        _merge_kernel,
        grid=(B, S // tm),
        in_specs=[
            pl.BlockSpec((1, tm, D), lambda b, i: (b, i, 0)),
            pl.BlockSpec((1, 2 * BRANCH_WIDTH, tm), lambda b, i: (b, 0, i)),
            _const_spec((1, D)),
            _const_spec((D, 2 * D)),
            _const_spec((1, 2 * D)),
            _const_spec((BRANCH_WIDTH, D)),
            _const_spec((BRANCH_WIDTH, D)),
            _const_spec((D, D)),
        ],
        out_specs=pl.BlockSpec((1, tm, D), lambda b, i: (b, i, 0)),
        out_shape=jax.ShapeDtypeStruct((B, S, D), F32),
        compiler_params=pltpu.CompilerParams(
            dimension_semantics=("arbitrary", "arbitrary"),
            vmem_limit_bytes=VMEM_LIMIT),
        name="merge",
    )(x, ot, attn_g, wg, bg, wa, wb, wo)


def _ffn_kernel(x_ref, g_ref, wup_ref, cw_ref, cb_ref, wdn_ref, y_ref, gs_scr, *, tm, dff):
    i = pl.program_id(1)

    @pl.when(i == 0)
    def _():
        gs_scr[0:8, :] = jnp.zeros((8, dff), F32)

    x = x_ref[0]
    ms = jnp.mean(x * x, axis=-1, keepdims=True)
    h = (x * lax.rsqrt(ms + NORM_EPS) * g_ref[...]).astype(BF16)
    up = jnp.dot(h, wup_ref[...], preferred_element_type=F32)
    u = up[:, :dff]
    g = up[:, dff:]
    gs_scr[8:8 + tm, :] = g
    g_m1 = gs_scr[7:7 + tm, :]
    g_m2 = gs_scr[6:6 + tm, :]
    gc = (cw_ref[2:3, :] * g + cw_ref[1:2, :] * g_m1 + cw_ref[0:1, :] * g_m2
          + cb_ref[...])
    gs_scr[0:8, :] = g[tm - 8:tm, :]
    act = (gc * jax.nn.sigmoid(gc) * u).astype(BF16)
    y_ref[0] = x + jnp.dot(act, wdn_ref[...], preferred_element_type=F32)


def _ffn(x, ffn_g, wup, cw, cb, wdn, *, tm):
    B, S, D = x.shape
    dff = wdn.shape[0]
    kern = functools.partial(_ffn_kernel, tm=tm, dff=dff)
    return pl.pallas_call(
        kern,
        grid=(B, S // tm),
        in_specs=[
            pl.BlockSpec((1, tm, D), lambda b, i: (b, i, 0)),
            _const_spec((1, D)),
            _const_spec((D, 2 * dff)),
            _const_spec((CONV_WIDTH, dff)),
            _const_spec((1, dff)),
            _const_spec((dff, D)),
        ],
        out_specs=pl.BlockSpec((1, tm, D), lambda b, i: (b, i, 0)),
        out_shape=jax.ShapeDtypeStruct((B, S, D), F32),
        scratch_shapes=[pltpu.VMEM((tm + 8, dff), F32)],
        compiler_params=pltpu.CompilerParams(
            dimension_semantics=("arbitrary", "arbitrary"),
            vmem_limit_bytes=VMEM_LIMIT),
        name="ffn",
    )(x, ffn_g, wup, cw, cb, wdn)


def _layer(x, attn_norm_g, w_in, b_forget, b_gate, moba_q_norm_g, moba_k_norm_g,
           fox_q_norm_g, fox_k_norm_g, w_branch_moba, w_branch_fox, w_out,
           ffn_norm_g, w_ffn_up, ffn_conv_w, ffn_conv_b, w_ffn_down):
    B, S, D = x.shape
    nqkv = 6 * BRANCH_WIDTH
    scale = HEAD_DIM ** -0.5

    w_f = jnp.pad(w_in[:, nqkv:nqkv + N_BRANCH_HEADS], ((0, 0), (0, AUG_ROWS - N_BRANCH_HEADS)))
    wt = jnp.concatenate([w_in[:, :nqkv], w_f], axis=1).T.astype(BF16)
    wg = w_in[:, nqkv + N_BRANCH_HEADS:].astype(BF16)
    gains = jnp.concatenate([moba_q_norm_g * scale, moba_k_norm_g,
                             fox_q_norm_g * scale, fox_k_norm_g])
    gains = jnp.broadcast_to(gains[:, None], (4 * HEAD_DIM, LANES)).astype(F32)
    bf = jnp.pad(b_forget.astype(F32), (0, AUG_ROWS - N_BRANCH_HEADS))
    bf = jnp.broadcast_to(bf[:, None], (AUG_ROWS, LANES))
    inv_freq = jnp.power(ROPE_THETA, -jnp.arange(ROPE_HALF, dtype=F32) * 2.0 / ROPE_DIM)
    ang = jnp.arange(S).astype(F32)[:, None] * inv_freq[None, :]
    cos_t, sin_t = jnp.cos(ang).T, jnp.sin(ang).T

    q, k, v = _proj(x, attn_norm_g.reshape(1, D), wt, gains, bf, cos_t, sin_t, tm=MOBA_BLOCK)
    ot = _attn(q, k, v, hg=8, t=MOBA_BLOCK)
    x1 = _merge(x, ot, attn_norm_g.reshape(1, D), wg, b_gate.reshape(1, 2 * D).astype(F32),
                w_branch_moba.astype(BF16), w_branch_fox.astype(BF16), w_out.astype(BF16),
                tm=512)
    return _ffn(x1, ffn_norm_g.reshape(1, D), w_ffn_up.astype(BF16), ffn_conv_w.astype(F32),
                ffn_conv_b.reshape(1, -1).astype(F32), w_ffn_down.astype(BF16), tm=256)


def kernel(x, attn_norm_g, w_in, b_forget, b_gate, moba_q_norm_g, moba_k_norm_g,
           fox_q_norm_g, fox_k_norm_g, w_branch_moba, w_branch_fox, w_out,
           ffn_norm_g, w_ffn_up, ffn_conv_w, ffn_conv_b, w_ffn_down):
    for l in range(attn_norm_g.shape[0]):
        x = _layer(x, attn_norm_g[l], w_in[l], b_forget[l], b_gate[l], moba_q_norm_g[l],
                   moba_k_norm_g[l], fox_q_norm_g[l], fox_k_norm_g[l], w_branch_moba[l],
                   w_branch_fox[l], w_out[l], ffn_norm_g[l], w_ffn_up[l], ffn_conv_w[l],
                   ffn_conv_b[l], w_ffn_down[l])
    return x
```

```python
import functools
import math

import jax
import jax.numpy as jnp
from jax import lax
from jax.experimental import pallas as pl
from jax.experimental.pallas import tpu as pltpu

HEAD_DIM = 64
N_BRANCH_HEADS = 8
N_HEADS = 2 * N_BRANCH_HEADS
BRANCH_WIDTH = N_BRANCH_HEADS * HEAD_DIM
MOBA_BLOCK = 256
MOBA_TOPK = 3
ROPE_THETA = 500000.0
ROPE_DIM = HEAD_DIM // 4
ROPE_HALF = ROPE_DIM // 2
CONV_WIDTH = 3
NORM_EPS = 1e-6
NEG_INF = -1e30
LOG2E = math.log2(math.e)

LANES = 128
SUBLANES = 8
MXU_DIM = 256
AUG_ROWS = 16
QK_ROWS = 128
V_ROWS = HEAD_DIM + 16
VMEM_LIMIT = 56 * 1024 * 1024

PROJ_TM = MOBA_BLOCK
ATTN_T = MOBA_BLOCK
MERGE_TM = 512
FFN_TM = 512
FFN_CHUNKS = 2

F32_EXP_ZERO = 104.0

F32 = jnp.float32
BF16 = jnp.bfloat16


def _split3(x):
    a = x.astype(BF16)
    r = x - a.astype(F32)
    b = r.astype(BF16)
    c = (r - b.astype(F32)).astype(BF16)
    return a, b, c


def _const_spec(shape):
    n = len(shape)
    return pl.BlockSpec(shape, lambda *_: (0,) * n, pipeline_mode=pl.Buffered(1))


def _rms_norm_bf16(x, g):
    ms = jnp.mean(x * x, axis=-1, keepdims=True)
    return (x * lax.rsqrt(ms + NORM_EPS) * g).astype(BF16)


def _proj_kernel(x_ref, g_ref, wt_ref, gain_ref, bf_ref, cos_ref, sin_ref,
                 q_ref, k_ref, v_ref, cinfo_ref, kmean_scr, carry_scr, *, tm):
    i = pl.program_id(1)

    @pl.when(i == 0)
    def _():
        kmean_scr[...] = jnp.zeros_like(kmean_scr)
        carry_scr[...] = jnp.zeros_like(carry_scr)

    h = _rms_norm_bf16(x_ref[0], g_ref[...])
    pt = lax.dot_general(wt_ref[...], h, (((1,), (1,)), ((), ())),
                         preferred_element_type=F32)

    cos = cos_ref[...]
    sin = sin_ref[...]
    row16 = lax.broadcasted_iota(jnp.int32, (AUG_ROWS, tm), 0)
    zeros_pad = jnp.zeros((QK_ROWS - HEAD_DIM - AUG_ROWS, tm), F32)
    ones_row = jnp.where(row16 == 0, 1.0, 0.0).astype(BF16)

    def tile_lanes(a):
        return jnp.concatenate([a] * (tm // LANES), axis=1)

    def head_norm(t, gain):
        m = jnp.mean(t * t, axis=0, keepdims=True)
        return t * lax.rsqrt(m + NORM_EPS) * tile_lanes(gain)

    def rope(t):
        x1 = t[0:ROPE_HALF]
        x2 = t[ROPE_HALF:ROPE_DIM]
        return jnp.concatenate(
            [x1 * cos - x2 * sin, x2 * cos + x1 * sin, t[ROPE_DIM:]], axis=0)

    def k_token_major(kt, aug):
        return jnp.concatenate([kt, aug, zeros_pad], axis=0).T

    def emit_q(hd, qt, aug):
        q_ref[0, hd, 0:HEAD_DIM, :] = qt.astype(BF16)
        q_ref[0, hd, HEAD_DIM:HEAD_DIM + AUG_ROWS, :] = aug.astype(BF16)
        q_ref[0, hd, HEAD_DIM + AUG_ROWS:, :] = zeros_pad.astype(BF16)

    def emit_v(hd, vt):
        v_ref[0, hd, 0, 0:HEAD_DIM, :] = vt.astype(BF16)
        v_ref[0, hd, 0, HEAD_DIM:, :] = ones_row

    def rows(base, hd):
        r0 = base + hd * HEAD_DIM
        return pt[r0:r0 + HEAD_DIM, :]

    gq_a = gain_ref[0 * HEAD_DIM:1 * HEAD_DIM, :]
    gk_a = gain_ref[1 * HEAD_DIM:2 * HEAD_DIM, :]
    gq_b = gain_ref[2 * HEAD_DIM:3 * HEAD_DIM, :]
    gk_b = gain_ref[3 * HEAD_DIM:4 * HEAD_DIM, :]

    own_onehot = jnp.where(row16 == i, 1.0, 0.0)
    row16f = row16.astype(F32)
    table_row = lax.broadcasted_iota(jnp.int32, (AUG_ROWS, QK_ROWS), 0)
    neg_inf = jnp.float32(-jnp.inf)
    head_lanes = lax.broadcasted_iota(jnp.int32, (1, QK_ROWS), 1) < HEAD_DIM
    zeros_head = jnp.zeros((QK_ROWS - HEAD_DIM, tm), F32)
    for hd in range(N_BRANCH_HEADS):
        qt = rope(head_norm(rows(0, hd), gq_a))
        kt = rope(head_norm(rows(BRANCH_WIDTH, hd), gk_a))
        kaug = k_token_major(kt, own_onehot)
        k_ref[0, hd] = kaug.astype(BF16)
        kmean = jnp.sum(kaug, axis=0, keepdims=True) * (1.0 / tm)
        kmean = jnp.where(head_lanes, kmean, 0.0)
        table = jnp.where(table_row == i, kmean, kmean_scr[hd])
        kmean_scr[hd] = table
        t_hi, t_lo, _ = _split3(table)
        q_hi, q_lo, _ = _split3(jnp.concatenate([qt, zeros_head], axis=0))
        gate = (jnp.dot(t_hi, q_hi, preferred_element_type=F32)
                + jnp.dot(t_hi, q_lo, preferred_element_type=F32)
                + jnp.dot(t_lo, q_hi, preferred_element_type=F32))
        gate = jnp.where(row16 < i, gate, neg_inf)
        sel = row16 == i
        for _ in range(MOBA_TOPK):
            mx = jnp.max(gate, axis=0, keepdims=True)
            is_mx = jnp.logical_and(gate == mx, mx > neg_inf)
            first = jnp.min(jnp.where(is_mx, row16f, float(AUG_ROWS)), axis=0, keepdims=True)
            pick = row16f == first
            sel = jnp.logical_or(sel, pick)
            gate = jnp.where(pick, neg_inf, gate)
        emit_q(hd, qt, jnp.where(sel, 0.0, NEG_INF))
        emit_v(hd, rows(2 * BRANCH_WIDTH, hd))

    z = pt[6 * BRANCH_WIDTH:6 * BRANCH_WIDTH + AUG_ROWS, :] + tile_lanes(bf_ref[...])
    logf = jnp.minimum(z, 0.0) - jnp.log1p(jnp.exp(-jnp.abs(z)))
    logf = jnp.where(row16 < N_BRANCH_HEADS, logf, 0.0)
    r_i = lax.broadcasted_iota(jnp.int32, (tm, tm), 0)
    c_i = lax.broadcasted_iota(jnp.int32, (tm, tm), 1)
    tri = jnp.where(r_i <= c_i, 1.0, 0.0).astype(BF16)
    ones_mat = jnp.ones((tm, LANES), BF16)
    parts = _split3(logf)
    c_loc = sum(jnp.dot(p, tri, preferred_element_type=F32) for p in parts)
    total = sum(jnp.dot(p, ones_mat, preferred_element_type=F32) for p in parts)
    carry = carry_scr[...]
    c = c_loc + tile_lanes(carry)
    carry_scr[...] = carry + total
    cinfo_ref[0, 0, 0] = c[:, 0:LANES]
    cinfo_ref[0, 0, 1] = c[:, tm - LANES:tm]
    c1, c2, c3 = (p.astype(F32) for p in _split3(c * LOG2E))

    def aug_rows(a, b, d, first_ones):
        o = 3 if first_ones else 0
        abd = jnp.where(row16 == o, a, jnp.where(row16 == o + 1, b, d))
        in_abd = jnp.logical_and(row16 >= o, row16 < o + 3)
        return jnp.where(in_abd, abd, jnp.where(row16 < 6, 1.0, 0.0))

    for hd in range(N_BRANCH_HEADS):
        qt = head_norm(rows(3 * BRANCH_WIDTH, hd), gq_b)
        kt = head_norm(rows(4 * BRANCH_WIDTH, hd), gk_b)
        a = jnp.broadcast_to(c1[hd:hd + 1], (AUG_ROWS, tm))
        b = jnp.broadcast_to(c2[hd:hd + 1], (AUG_ROWS, tm))
        d = jnp.broadcast_to(c3[hd:hd + 1], (AUG_ROWS, tm))
        emit_q(N_BRANCH_HEADS + hd, qt, aug_rows(a, b, d, first_ones=False))
        k_ref[0, N_BRANCH_HEADS + hd] = k_token_major(
            kt, aug_rows(-a, -b, -d, first_ones=True)).astype(BF16)
        emit_v(N_BRANCH_HEADS + hd, rows(5 * BRANCH_WIDTH, hd))


def _proj(x, attn_g, wt, gains, bf, cos_t, sin_t):
    B, S, D = x.shape
    tm = PROJ_TM
    ns = S // tm
    rows = wt.shape[0]
    kern = functools.partial(_proj_kernel, tm=tm)
    return pl.pallas_call(
        kern,
        grid=(B, ns),
        in_specs=[
            pl.BlockSpec((1, tm, D), lambda b, i: (b, i, 0)),
            _const_spec((1, D)),
            _const_spec((rows, D)),
            _const_spec((4 * HEAD_DIM, LANES)),
            _const_spec((AUG_ROWS, LANES)),
            pl.BlockSpec((ROPE_HALF, tm), lambda b, i: (0, i)),
            pl.BlockSpec((ROPE_HALF, tm), lambda b, i: (0, i)),
        ],
        out_specs=[
            pl.BlockSpec((1, N_HEADS, QK_ROWS, tm), lambda b, i: (b, 0, 0, i)),
            pl.BlockSpec((1, N_HEADS, tm, QK_ROWS), lambda b, i: (b, 0, i, 0)),
            pl.BlockSpec((1, N_HEADS, 1, V_ROWS, tm), lambda b, i: (b, 0, i, 0, 0)),
            pl.BlockSpec((1, 1, 2, AUG_ROWS, LANES), lambda b, i: (b, i, 0, 0, 0)),
        ],
        out_shape=[
            jax.ShapeDtypeStruct((B, N_HEADS, QK_ROWS, S), BF16),
            jax.ShapeDtypeStruct((B, N_HEADS, S, QK_ROWS), BF16),
            jax.ShapeDtypeStruct((B, N_HEADS, ns, V_ROWS, tm), BF16),
            jax.ShapeDtypeStruct((B, ns, 2, AUG_ROWS, LANES), F32),
        ],
        scratch_shapes=[
            pltpu.VMEM((N_BRANCH_HEADS, AUG_ROWS, QK_ROWS), F32),
            pltpu.VMEM((AUG_ROWS, LANES), F32),
        ],
        compiler_params=pltpu.CompilerParams(
            dimension_semantics=("arbitrary", "arbitrary"),
            vmem_limit_bytes=VMEM_LIMIT),
        name="proj",
    )(x, attn_g, wt, gains, bf, cos_t, sin_t)


def _attn_kernel(cf_ref, cl_ref, thr_ref, q_ref, k_ref, v_ref, o_ref, m_scr, acc_scr,
                 *, hg, t, nq):
    b = pl.program_id(0)
    g = pl.program_id(1)
    qi = pl.program_id(2)

    thr = thr_ref[0]
    lo = jnp.int32(nq)
    for h in range(N_BRANCH_HEADS):
        base = (b * N_BRANCH_HEADS + h) * nq
        c_q = cf_ref[base + qi]
        ok = jnp.int32(1)
        cnt = jnp.int32(0)
        for kj in range(nq):
            skip = jnp.logical_and(kj < qi, c_q - cl_ref[base + kj] < -thr)
            ok = ok * skip.astype(jnp.int32)
            cnt = cnt + ok
        lo = jnp.minimum(lo, cnt)
    lo = jnp.where(g == 1, lo, 0)

    def scores(hd, kj):
        k = k_ref[0, hd, pl.ds(pl.multiple_of(kj * t, t), t), :]
        return jnp.dot(k, q_ref[0, hd], preferred_element_type=F32)

    def accumulate(hd, ss, kjs):
        m_old = m_scr[hd]
        m_new = m_old
        for s in ss:
            m_new = jnp.maximum(m_new, jnp.max(s, axis=0, keepdims=True))
        alpha = jnp.exp2(m_old - m_new)
        pv = None
        for s, kj in zip(ss, kjs):
            p = jnp.exp2(s - m_new).astype(BF16)
            d = jnp.dot(v_ref[0, hd, kj], p, preferred_element_type=F32)
            pv = d if pv is None else pv + d
        acc_scr[hd] = alpha * acc_scr[hd] + pv
        m_scr[hd] = m_new

    def tiles(kjs, last_mask):
        ss = [[scores(hd, kj) for kj in kjs] for hd in range(hg)]
        for hd in range(hg):
            if last_mask is not None:
                ss[hd][-1] = jnp.where(last_mask, ss[hd][-1], -jnp.inf)
            accumulate(hd, ss[hd], kjs)

    for hd in range(hg):
        m_scr[hd] = jnp.full((1, t), -jnp.inf, F32)
        acc_scr[hd] = jnp.zeros((V_ROWS, t), F32)

    n_off = qi - lo

    def body(j, carry):
        kj = lo + 2 * j
        tiles([kj, kj + 1], None)
        return carry

    lax.fori_loop(0, lax.shift_right_logical(n_off, 1), body, 0)

    key_i = lax.broadcasted_iota(jnp.int32, (t, t), 0)
    qry_i = lax.broadcasted_iota(jnp.int32, (t, t), 1)
    causal = key_i <= qry_i
    odd = (n_off & 1) == 1

    @pl.when(odd)
    def _():
        tiles([qi - 1, qi], causal)

    @pl.when(jnp.logical_not(odd))
    def _():
        tiles([qi], causal)

    for hd in range(hg):
        acc = acc_scr[hd]
        o = acc[0:HEAD_DIM] / acc[HEAD_DIM:HEAD_DIM + 1]
        o_ref[0, hd * HEAD_DIM:(hd + 1) * HEAD_DIM, :] = o.astype(BF16)


def _attn(c_first, c_last, thr, q, k, v):
    B, H, _, S = q.shape
    t = ATTN_T
    hg = N_BRANCH_HEADS
    nq = S // t
    kern = functools.partial(_attn_kernel, hg=hg, t=t, nq=nq)
    grid_spec = pltpu.PrefetchScalarGridSpec(
        num_scalar_prefetch=3,
        grid=(B, H // hg, nq),
        in_specs=[
            pl.BlockSpec((1, hg, QK_ROWS, t), lambda b, g, i, *_: (b, g, 0, i)),
            pl.BlockSpec((1, hg, S, QK_ROWS), lambda b, g, i, *_: (b, g, 0, 0)),
            pl.BlockSpec((1, hg, nq, V_ROWS, t), lambda b, g, i, *_: (b, g, 0, 0, 0)),
        ],
        out_specs=pl.BlockSpec((1, hg * HEAD_DIM, t), lambda b, g, i, *_: (b, g, i)),
        scratch_shapes=[
            pltpu.VMEM((hg, 1, t), F32),
            pltpu.VMEM((hg, V_ROWS, t), F32),
        ],
    )
    return pl.pallas_call(
        kern,
        grid_spec=grid_spec,
        out_shape=jax.ShapeDtypeStruct((B, H * HEAD_DIM, S), BF16),
        compiler_params=pltpu.CompilerParams(
            dimension_semantics=("arbitrary", "arbitrary", "arbitrary"),
            vmem_limit_bytes=VMEM_LIMIT),
        name="attn",
    )(c_first, c_last, thr, q, k, v)


def _merge_kernel(x_ref, ot_ref, g_ref, wg_ref, bg_ref, wa_ref, wb_ref, wo_ref, y_ref):
    x = x_ref[0]
    d = x.shape[-1]
    h = _rms_norm_bf16(x, g_ref[...])
    glog = jnp.dot(h, wg_ref[...], preferred_element_type=F32) + bg_ref[...]
    gates = jax.nn.sigmoid(glog)
    dn = (((0,), (0,)), ((), ()))
    br_a = lax.dot_general(ot_ref[0, 0:BRANCH_WIDTH, :], wa_ref[...], dn,
                           preferred_element_type=F32)
    br_b = lax.dot_general(ot_ref[0, BRANCH_WIDTH:, :], wb_ref[...], dn,
                           preferred_element_type=F32)
    merged = (gates[:, :d] * br_a + gates[:, d:] * br_b).astype(BF16)
    y_ref[0] = x + jnp.dot(merged, wo_ref[...], preferred_element_type=F32)


def _merge(x, ot, attn_g, wg, bg, wa, wb, wo):
    B, S, D = x.shape
    tm = MERGE_TM
    return pl.pallas_call(
        _merge_kernel,
        grid=(B, S // tm),
        in_specs=[
            pl.BlockSpec((1, tm, D), lambda b, i: (b, i, 0)),
            pl.BlockSpec((1, 2 * BRANCH_WIDTH, tm), lambda b, i: (b, 0, i)),
            _const_spec((1, D)),
            _const_spec((D, 2 * D)),
            _const_spec((1, 2 * D)),
            _const_spec((BRANCH_WIDTH, D)),
            _const_spec((BRANCH_WIDTH, D)),
            _const_spec((D, D)),
        ],
        out_specs=pl.BlockSpec((1, tm, D), lambda b, i: (b, i, 0)),
        out_shape=jax.ShapeDtypeStruct((B, S, D), F32),
        compiler_params=pltpu.CompilerParams(
            dimension_semantics=("arbitrary", "arbitrary"),
            vmem_limit_bytes=VMEM_LIMIT),
        name="merge",
    )(x, ot, attn_g, wg, bg, wa, wb, wo)


def _ffn_kernel(x_ref, g_ref, wup_ref, cw_ref, cb_ref, wdn_ref, y_ref, gs_scr,
                *, tm, dff, chunks):
    i = pl.program_id(1)

    @pl.when(i == 0)
    def _():
        gs_scr[0:SUBLANES, :] = jnp.zeros((SUBLANES, dff), F32)

    x = x_ref[0]
    h = _rms_norm_bf16(x, g_ref[...])
    n_tiles = dff // MXU_DIM
    bounds = [MXU_DIM * ((n_tiles * c + chunks - 1) // chunks) for c in range(chunks)] + [dff]
    y = x
    for c in range(chunks):
        lo, hi = bounds[c], bounds[c + 1]
        u = jnp.dot(h, wup_ref[:, lo:hi], preferred_element_type=F32)
        g = jnp.dot(h, wup_ref[:, dff + lo:dff + hi], preferred_element_type=F32)
        gs_scr[SUBLANES:SUBLANES + tm, lo:hi] = g
        g_m1 = gs_scr[SUBLANES - 1:SUBLANES - 1 + tm, lo:hi]
        g_m2 = gs_scr[SUBLANES - 2:SUBLANES - 2 + tm, lo:hi]
        gc = (cw_ref[2:3, lo:hi] * g + cw_ref[1:2, lo:hi] * g_m1 + cw_ref[0:1, lo:hi] * g_m2
              + cb_ref[:, lo:hi])
        gs_scr[0:SUBLANES, lo:hi] = g[tm - SUBLANES:tm, :]
        act = (gc * jax.nn.sigmoid(gc) * u).astype(BF16)
        y = y + jnp.dot(act, wdn_ref[lo:hi, :], preferred_element_type=F32)
    y_ref[0] = y


def _ffn(x, ffn_g, wup, cw, cb, wdn):
    B, S, D = x.shape
    tm = FFN_TM
    dff = wdn.shape[0]
    kern = functools.partial(_ffn_kernel, tm=tm, dff=dff, chunks=FFN_CHUNKS)
    return pl.pallas_call(
        kern,
        grid=(B, S // tm),
        in_specs=[
            pl.BlockSpec((1, tm, D), lambda b, i: (b, i, 0)),
            _const_spec((1, D)),
            _const_spec((D, 2 * dff)),
            _const_spec((CONV_WIDTH, dff)),
            _const_spec((1, dff)),
            _const_spec((dff, D)),
        ],
        out_specs=pl.BlockSpec((1, tm, D), lambda b, i: (b, i, 0)),
        out_shape=jax.ShapeDtypeStruct((B, S, D), F32),
        scratch_shapes=[pltpu.VMEM((tm + SUBLANES, dff), F32)],
        compiler_params=pltpu.CompilerParams(
            dimension_semantics=("arbitrary", "arbitrary"),
            vmem_limit_bytes=VMEM_LIMIT),
        name="ffn",
    )(x, ffn_g, wup, cw, cb, wdn)


def _layer(x, attn_norm_g, w_in, b_forget, b_gate, moba_q_norm_g, moba_k_norm_g,
           fox_q_norm_g, fox_k_norm_g, w_branch_moba, w_branch_fox, w_out,
           ffn_norm_g, w_ffn_up, ffn_conv_w, ffn_conv_b, w_ffn_down):
    B, S, D = x.shape
    assert S % max(PROJ_TM, ATTN_T, MERGE_TM, FFN_TM) == 0 and S // MOBA_BLOCK <= AUG_ROWS
    nqkv = 6 * BRANCH_WIDTH
    scale = HEAD_DIM ** -0.5
    q_scale = scale * LOG2E

    w_f = jnp.pad(w_in[:, nqkv:nqkv + N_BRANCH_HEADS], ((0, 0), (0, AUG_ROWS - N_BRANCH_HEADS)))
    wt = jnp.concatenate([w_in[:, :nqkv], w_f], axis=1).T.astype(BF16)
    wg = w_in[:, nqkv + N_BRANCH_HEADS:].astype(BF16)
    gains = jnp.concatenate([moba_q_norm_g * q_scale, moba_k_norm_g,
                             fox_q_norm_g * q_scale, fox_k_norm_g])
    gains = jnp.broadcast_to(gains[:, None], (4 * HEAD_DIM, LANES)).astype(F32)
    bf = jnp.pad(b_forget.astype(F32), (0, AUG_ROWS - N_BRANCH_HEADS))
    bf = jnp.broadcast_to(bf[:, None], (AUG_ROWS, LANES))
    inv_freq = jnp.power(ROPE_THETA, -jnp.arange(ROPE_HALF, dtype=F32) * 2.0 / ROPE_DIM)
    ang = jnp.arange(S).astype(F32)[:, None] * inv_freq[None, :]
    cos_t, sin_t = jnp.cos(ang).T, jnp.sin(ang).T
    qk_bound = 1.01 * math.sqrt(HEAD_DIM) * jnp.max(jnp.abs(fox_q_norm_g)) * jnp.max(jnp.abs(fox_k_norm_g))
    thr = (F32_EXP_ZERO + 2.0 * qk_bound).astype(F32).reshape(1)

    q, k, v, cinfo = _proj(x, attn_norm_g.reshape(1, D), wt, gains, bf, cos_t, sin_t)
    c_first = cinfo[:, :, 0, :N_BRANCH_HEADS, 0].transpose(0, 2, 1).reshape(-1)
    c_last = cinfo[:, :, 1, :N_BRANCH_HEADS, LANES - 1].transpose(0, 2, 1).reshape(-1)
    ot = _attn(c_first, c_last, thr, q, k, v)
    x1 = _merge(x, ot, attn_norm_g.reshape(1, D), wg, b_gate.reshape(1, 2 * D).astype(F32),
                w_branch_moba.astype(BF16), w_branch_fox.astype(BF16), w_out.astype(BF16))
    return _ffn(x1, ffn_norm_g.reshape(1, D), w_ffn_up.astype(BF16), ffn_conv_w.astype(F32),
                ffn_conv_b.reshape(1, -1).astype(F32), w_ffn_down.astype(BF16))


def kernel(x, attn_norm_g, w_in, b_forget, b_gate, moba_q_norm_g, moba_k_norm_g,
           fox_q_norm_g, fox_k_norm_g, w_branch_moba, w_branch_fox, w_out,
           ffn_norm_g, w_ffn_up, ffn_conv_w, ffn_conv_b, w_ffn_down):
    for l in range(attn_norm_g.shape[0]):
        x = _layer(x, attn_norm_g[l], w_in[l], b_forget[l], b_gate[l], moba_q_norm_g[l],
                   moba_k_norm_g[l], fox_q_norm_g[l], fox_k_norm_g[l], w_branch_moba[l],
                   w_branch_fox[l], w_out[l], ffn_norm_g[l], w_ffn_up[l], ffn_conv_w[l],
                   ffn_conv_b[l], w_ffn_down[l])
    return x
```

```python
import functools
import math

import jax
import jax.numpy as jnp
from jax import lax
from jax.experimental import pallas as pl
from jax.experimental.pallas import tpu as pltpu

HEAD_DIM = 64
N_BRANCH_HEADS = 8
N_HEADS = 2 * N_BRANCH_HEADS
BRANCH_WIDTH = N_BRANCH_HEADS * HEAD_DIM
MOBA_BLOCK = 256
MOBA_TOPK = 3
ROPE_THETA = 500000.0
ROPE_DIM = HEAD_DIM // 4
ROPE_HALF = ROPE_DIM // 2
CONV_WIDTH = 3
NORM_EPS = 1e-6
NEG_INF = -1e30
LOG2E = math.log2(math.e)

LANES = 128
SUBLANES = 8
MXU_DIM = 256
AUG_ROWS = 16
QK_ROWS = 128
V_ROWS = HEAD_DIM + 16
VMEM_LIMIT = 56 * 1024 * 1024

PROJ_TM = MOBA_BLOCK
PROJ_SUBTILES = 2
ATTN_T = MOBA_BLOCK
MERGE_TM = 512
FFN_TM = 512
FFN_CHUNKS = 2

F32_EXP_ZERO = 104.0
MAX_OFFSET_SPAN = 100.0

F32 = jnp.float32
BF16 = jnp.bfloat16


def _split3(x):
    a = x.astype(BF16)
    r = x - a.astype(F32)
    b = r.astype(BF16)
    c = (r - b.astype(F32)).astype(BF16)
    return a, b, c


def _const_spec(shape):
    n = len(shape)
    return pl.BlockSpec(shape, lambda *_: (0,) * n, pipeline_mode=pl.Buffered(1))


def _rms_norm_bf16(x, g):
    ms = jnp.mean(x * x, axis=-1, keepdims=True)
    return (x * lax.rsqrt(ms + NORM_EPS) * g).astype(BF16)


def _proj_kernel(x_ref, g_ref, wt_ref, gain_ref, bf_ref, off_ref, cos_ref, sin_ref,
                 q_ref, k_ref, v_ref, cinfo_ref, kmean_scr, carry_scr, *, tm, nsub):
    step = pl.program_id(1)

    @pl.when(step == 0)
    def _():
        kmean_scr[...] = jnp.zeros_like(kmean_scr)
        carry_scr[...] = jnp.zeros_like(carry_scr)

    pts = []
    for sub in range(nsub):
        h = _rms_norm_bf16(x_ref[0, sub * tm:(sub + 1) * tm, :], g_ref[...])
        pts.append(lax.dot_general(wt_ref[...], h, (((1,), (1,)), ((), ())),
                                   preferred_element_type=F32))
    for sub in range(nsub):
        _proj_epilogue(pts[sub], step * nsub + sub, sub, gain_ref, bf_ref, off_ref, cos_ref, sin_ref,
                       q_ref, k_ref, v_ref, cinfo_ref, kmean_scr, carry_scr, tm=tm)


def _proj_epilogue(pt, i, sub, gain_ref, bf_ref, off_ref, cos_ref, sin_ref,
                   q_ref, k_ref, v_ref, cinfo_ref, kmean_scr, carry_scr, *, tm):
    tok = slice(sub * tm, (sub + 1) * tm)
    cos = cos_ref[:, tok]
    sin = sin_ref[:, tok]
    row16 = lax.broadcasted_iota(jnp.int32, (AUG_ROWS, tm), 0)
    pad_rows = QK_ROWS - HEAD_DIM - AUG_ROWS
    row8 = lax.broadcasted_iota(jnp.int32, (SUBLANES, tm), 0)
    pad_tail = jnp.zeros((pad_rows - SUBLANES, tm), F32)
    ones_row = jnp.where(row16 == 0, 1.0, 0.0).astype(BF16)

    def tile_lanes(a):
        return jnp.concatenate([a] * (tm // LANES), axis=1)

    q_pad = jnp.concatenate([jnp.where(row8 == 0, tile_lanes(off_ref[...]), 0.0), pad_tail], axis=0)
    k_pad = jnp.concatenate([jnp.where(row8 == 0, 1.0, 0.0), pad_tail], axis=0)

    def head_norm(t, gain):
        m = jnp.mean(t * t, axis=0, keepdims=True)
        return t * lax.rsqrt(m + NORM_EPS) * tile_lanes(gain)

    def rope(t):
        x1 = t[0:ROPE_HALF]
        x2 = t[ROPE_HALF:ROPE_DIM]
        return jnp.concatenate(
            [x1 * cos - x2 * sin, x2 * cos + x1 * sin, t[ROPE_DIM:]], axis=0)

    def k_token_major(kt, aug):
        return jnp.concatenate([kt, aug, k_pad], axis=0).T

    def emit_q(hd, qt, aug):
        q_ref[0, hd, 0:HEAD_DIM, tok] = qt.astype(BF16)
        q_ref[0, hd, HEAD_DIM:HEAD_DIM + AUG_ROWS, tok] = aug.astype(BF16)
        q_ref[0, hd, HEAD_DIM + AUG_ROWS:, tok] = q_pad.astype(BF16)

    def emit_k(hd, kaug):
        k_ref[0, hd, tok, :] = kaug.astype(BF16)

    def emit_v(hd, vt):
        v_ref[0, hd, sub, 0:HEAD_DIM, :] = vt.astype(BF16)
        v_ref[0, hd, sub, HEAD_DIM:, :] = ones_row

    def rows(base, hd):
        r0 = base + hd * HEAD_DIM
        return pt[r0:r0 + HEAD_DIM, :]

    gq_a = gain_ref[0 * HEAD_DIM:1 * HEAD_DIM, :]
    gk_a = gain_ref[1 * HEAD_DIM:2 * HEAD_DIM, :]
    gq_b = gain_ref[2 * HEAD_DIM:3 * HEAD_DIM, :]
    gk_b = gain_ref[3 * HEAD_DIM:4 * HEAD_DIM, :]

    own_onehot = jnp.where(row16 == i, 1.0, 0.0)
    row16f = row16.astype(F32)
    table_row = lax.broadcasted_iota(jnp.int32, (AUG_ROWS, QK_ROWS), 0)
    neg_inf = jnp.float32(-jnp.inf)
    head_lanes = lax.broadcasted_iota(jnp.int32, (1, QK_ROWS), 1) < HEAD_DIM
    zeros_head = jnp.zeros((QK_ROWS - HEAD_DIM, tm), F32)
    for hd in range(N_BRANCH_HEADS):
        qt = rope(head_norm(rows(0, hd), gq_a))
        kt = rope(head_norm(rows(BRANCH_WIDTH, hd), gk_a))
        kaug = k_token_major(kt, own_onehot)
        emit_k(hd, kaug)
        kmean = jnp.sum(kaug, axis=0, keepdims=True) * (1.0 / tm)
        kmean = jnp.where(head_lanes, kmean, 0.0)
        table = jnp.where(table_row == i, kmean, kmean_scr[hd])
        kmean_scr[hd] = table
        t_hi, t_lo, _ = _split3(table)
        q_hi, q_lo, _ = _split3(jnp.concatenate([qt, zeros_head], axis=0))
        gate = (jnp.dot(t_hi, q_hi, preferred_element_type=F32)
                + jnp.dot(t_hi, q_lo, preferred_element_type=F32)
                + jnp.dot(t_lo, q_hi, preferred_element_type=F32))
        gate = jnp.where(row16 < i, gate, neg_inf)
        sel = row16 == i
        for _ in range(MOBA_TOPK):
            mx = jnp.max(gate, axis=0, keepdims=True)
            is_mx = jnp.logical_and(gate == mx, mx > neg_inf)
            first = jnp.min(jnp.where(is_mx, row16f, float(AUG_ROWS)), axis=0, keepdims=True)
            pick = row16f == first
            sel = jnp.logical_or(sel, pick)
            gate = jnp.where(pick, neg_inf, gate)
        emit_q(hd, qt, jnp.where(sel, 0.0, NEG_INF))
        emit_v(hd, rows(2 * BRANCH_WIDTH, hd))

    z = pt[6 * BRANCH_WIDTH:6 * BRANCH_WIDTH + AUG_ROWS, :] + tile_lanes(bf_ref[...])
    logf = jnp.minimum(z, 0.0) - jnp.log1p(jnp.exp(-jnp.abs(z)))
    logf = jnp.where(row16 < N_BRANCH_HEADS, logf, 0.0)
    r_i = lax.broadcasted_iota(jnp.int32, (tm, tm), 0)
    c_i = lax.broadcasted_iota(jnp.int32, (tm, tm), 1)
    tri = jnp.where(r_i <= c_i, 1.0, 0.0).astype(BF16)
    ones_mat = jnp.ones((tm, LANES), BF16)
    parts = _split3(logf)
    c_loc = sum(jnp.dot(p, tri, preferred_element_type=F32) for p in parts)
    total = sum(jnp.dot(p, ones_mat, preferred_element_type=F32) for p in parts)
    carry = carry_scr[...]
    c = c_loc + tile_lanes(carry)
    carry_scr[...] = carry + total
    cinfo_ref[0, sub, 0] = c[:, 0:LANES]
    cinfo_ref[0, sub, 1] = c[:, tm - LANES:tm]
    c1, c2, c3 = (p.astype(F32) for p in _split3(c * LOG2E))

    def aug_rows(a, b, d, first_ones):
        o = 3 if first_ones else 0
        abd = jnp.where(row16 == o, a, jnp.where(row16 == o + 1, b, d))
        in_abd = jnp.logical_and(row16 >= o, row16 < o + 3)
        return jnp.where(in_abd, abd, jnp.where(row16 < 6, 1.0, 0.0))

    for hd in range(N_BRANCH_HEADS):
        qt = head_norm(rows(3 * BRANCH_WIDTH, hd), gq_b)
        kt = head_norm(rows(4 * BRANCH_WIDTH, hd), gk_b)
        a = jnp.broadcast_to(c1[hd:hd + 1], (AUG_ROWS, tm))
        b = jnp.broadcast_to(c2[hd:hd + 1], (AUG_ROWS, tm))
        d = jnp.broadcast_to(c3[hd:hd + 1], (AUG_ROWS, tm))
        emit_q(N_BRANCH_HEADS + hd, qt, aug_rows(a, b, d, first_ones=False))
        emit_k(N_BRANCH_HEADS + hd, k_token_major(kt, aug_rows(-a, -b, -d, first_ones=True)))
        emit_v(N_BRANCH_HEADS + hd, rows(5 * BRANCH_WIDTH, hd))


def _proj(x, attn_g, wt, gains, bf, off, cos_t, sin_t):
    B, S, D = x.shape
    tm = PROJ_TM
    nsub = PROJ_SUBTILES
    ns = S // tm
    rows = wt.shape[0]
    kern = functools.partial(_proj_kernel, tm=tm, nsub=nsub)
    return pl.pallas_call(
        kern,
        grid=(B, ns // nsub),
        in_specs=[
            pl.BlockSpec((1, nsub * tm, D), lambda b, i: (b, i, 0)),
            _const_spec((1, D)),
            _const_spec((rows, D)),
            _const_spec((4 * HEAD_DIM, LANES)),
            _const_spec((AUG_ROWS, LANES)),
            _const_spec((SUBLANES, LANES)),
            pl.BlockSpec((ROPE_HALF, nsub * tm), lambda b, i: (0, i)),
            pl.BlockSpec((ROPE_HALF, nsub * tm), lambda b, i: (0, i)),
        ],
        out_specs=[
            pl.BlockSpec((1, N_HEADS, QK_ROWS, nsub * tm), lambda b, i: (b, 0, 0, i)),
            pl.BlockSpec((1, N_HEADS, nsub * tm, QK_ROWS), lambda b, i: (b, 0, i, 0)),
            pl.BlockSpec((1, N_HEADS, nsub, V_ROWS, tm), lambda b, i: (b, 0, i, 0, 0)),
            pl.BlockSpec((1, nsub, 2, AUG_ROWS, LANES), lambda b, i: (b, i, 0, 0, 0)),
        ],
        out_shape=[
            jax.ShapeDtypeStruct((B, N_HEADS, QK_ROWS, S), BF16),
            jax.ShapeDtypeStruct((B, N_HEADS, S, QK_ROWS), BF16),
            jax.ShapeDtypeStruct((B, N_HEADS, ns, V_ROWS, tm), BF16),
            jax.ShapeDtypeStruct((B, ns, 2, AUG_ROWS, LANES), F32),
        ],
        scratch_shapes=[
            pltpu.VMEM((N_BRANCH_HEADS, AUG_ROWS, QK_ROWS), F32),
            pltpu.VMEM((AUG_ROWS, LANES), F32),
        ],
        compiler_params=pltpu.CompilerParams(
            dimension_semantics=("arbitrary", "arbitrary"),
            vmem_limit_bytes=VMEM_LIMIT),
        name="proj",
    )(x, attn_g, wt, gains, bf, off, cos_t, sin_t)


def _attn_kernel(cf_ref, cl_ref, thr_ref, flag_ref, q_ref, k_ref, v_ref, o_ref, m_scr, acc_scr,
                 *, hg, t, nq):
    b = pl.program_id(0)
    g = pl.program_id(1)
    qi = pl.program_id(2)

    thr = thr_ref[0]
    lo = jnp.int32(nq)
    for h in range(N_BRANCH_HEADS):
        base = (b * N_BRANCH_HEADS + h) * nq
        c_q = cf_ref[base + qi]
        ok = jnp.int32(1)
        cnt = jnp.int32(0)
        for kj in range(nq):
            skip = jnp.logical_and(kj < qi, c_q - cl_ref[base + kj] < -thr)
            ok = ok * skip.astype(jnp.int32)
            cnt = cnt + ok
        lo = jnp.minimum(lo, cnt)
    lo = jnp.where(g == 1, lo, 0)

    def scores(hd, kj):
        k = k_ref[0, hd, pl.ds(pl.multiple_of(kj * t, t), t), :]
        return jnp.dot(k, q_ref[0, hd], preferred_element_type=F32)

    def pv_sum(hd, ps, kjs):
        pv = None
        for p, kj in zip(ps, kjs):
            d = jnp.dot(v_ref[0, hd, kj], p, preferred_element_type=F32)
            pv = d if pv is None else pv + d
        return pv

    def accumulate_bounded(hd, ss, kjs):
        ps = [jnp.exp2(s).astype(BF16) for s in ss]
        acc_scr[hd] = acc_scr[hd] + pv_sum(hd, ps, kjs)

    def accumulate_running_max(hd, ss, kjs):
        m_old = m_scr[hd]
        m_new = m_old
        for s in ss:
            m_new = jnp.maximum(m_new, jnp.max(s, axis=0, keepdims=True))
        alpha = jnp.exp2(m_old - m_new)
        ps = [jnp.exp2(s - m_new).astype(BF16) for s in ss]
        acc_scr[hd] = alpha * acc_scr[hd] + pv_sum(hd, ps, kjs)
        m_scr[hd] = m_new

    def tiles(kjs, last_mask, accumulate):
        ss = [[scores(hd, kj) for kj in kjs] for hd in range(hg)]
        for hd in range(hg):
            if last_mask is not None:
                ss[hd][-1] = jnp.where(last_mask, ss[hd][-1], -jnp.inf)
            accumulate(hd, ss[hd], kjs)

    for hd in range(hg):
        acc_scr[hd] = jnp.zeros((V_ROWS, t), F32)

    n_off = qi - lo
    key_i = lax.broadcasted_iota(jnp.int32, (t, t), 0)
    qry_i = lax.broadcasted_iota(jnp.int32, (t, t), 1)
    causal = key_i <= qry_i
    bounded = flag_ref[0] == 1

    @pl.when(bounded)
    def _():
        def body(j, carry):
            kj = lo + 2 * j
            tiles([kj, kj + 1], None, accumulate_bounded)
            return carry

        lax.fori_loop(0, lax.shift_right_logical(n_off, 1), body, 0)
        odd = (n_off & 1) == 1

        @pl.when(odd)
        def _():
            tiles([qi - 1, qi], causal, accumulate_bounded)

        @pl.when(jnp.logical_not(odd))
        def _():
            tiles([qi], causal, accumulate_bounded)

    @pl.when(jnp.logical_not(bounded))
    def _():
        for hd in range(hg):
            m_scr[hd] = jnp.full((1, t), -jnp.inf, F32)

        def body(j, carry):
            tiles([lo + j], None, accumulate_running_max)
            return carry

        lax.fori_loop(0, n_off, body, 0)
        tiles([qi], causal, accumulate_running_max)

    for hd in range(hg):
        acc = acc_scr[hd]
        o = acc[0:HEAD_DIM] / acc[HEAD_DIM:HEAD_DIM + 1]
        o_ref[0, hd * HEAD_DIM:(hd + 1) * HEAD_DIM, :] = o.astype(BF16)


def _attn(c_first, c_last, thr, flag, q, k, v):
    B, H, _, S = q.shape
    t = ATTN_T
    hg = N_BRANCH_HEADS
    nq = S // t
    kern = functools.partial(_attn_kernel, hg=hg, t=t, nq=nq)
    grid_spec = pltpu.PrefetchScalarGridSpec(
        num_scalar_prefetch=4,
        grid=(B, H // hg, nq),
        in_specs=[
            pl.BlockSpec((1, hg, QK_ROWS, t), lambda b, g, i, *_: (b, g, 0, i)),
            pl.BlockSpec((1, hg, S, QK_ROWS), lambda b, g, i, *_: (b, g, 0, 0)),
            pl.BlockSpec((1, hg, nq, V_ROWS, t), lambda b, g, i, *_: (b, g, 0, 0, 0)),
        ],
        out_specs=pl.BlockSpec((1, hg * HEAD_DIM, t), lambda b, g, i, *_: (b, g, i)),
        scratch_shapes=[
            pltpu.VMEM((hg, 1, t), F32),
            pltpu.VMEM((hg, V_ROWS, t), F32),
        ],
    )
    return pl.pallas_call(
        kern,
        grid_spec=grid_spec,
        out_shape=jax.ShapeDtypeStruct((B, H * HEAD_DIM, S), BF16),
        compiler_params=pltpu.CompilerParams(
            dimension_semantics=("arbitrary", "arbitrary", "arbitrary"),
            vmem_limit_bytes=VMEM_LIMIT),
        name="attn",
    )(c_first, c_last, thr, flag, q, k, v)


def _merge_kernel(x_ref, ot_ref, g_ref, wg_ref, bg_ref, wa_ref, wb_ref, wo_ref, y_ref):
    x = x_ref[0]
    d = x.shape[-1]
    h = _rms_norm_bf16(x, g_ref[...])
    glog = jnp.dot(h, wg_ref[...], preferred_element_type=F32) + bg_ref[...]
    gates = jax.nn.sigmoid(glog)
    dn = (((0,), (0,)), ((), ()))
    br_a = lax.dot_general(ot_ref[0, 0:BRANCH_WIDTH, :], wa_ref[...], dn,
                           preferred_element_type=F32)
    br_b = lax.dot_general(ot_ref[0, BRANCH_WIDTH:, :], wb_ref[...], dn,
                           preferred_element_type=F32)
    merged = (gates[:, :d] * br_a + gates[:, d:] * br_b).astype(BF16)
    y_ref[0] = x + jnp.dot(merged, wo_ref[...], preferred_element_type=F32)


def _merge(x, ot, attn_g, wg, bg, wa, wb, wo):
    B, S, D = x.shape
    tm = MERGE_TM
    return pl.pallas_call(
        _merge_kernel,
        grid=(B, S // tm),
        in_specs=[
            pl.BlockSpec((1, tm, D), lambda b, i: (b, i, 0)),
            pl.BlockSpec((1, 2 * BRANCH_WIDTH, tm), lambda b, i: (b, 0, i)),
            _const_spec((1, D)),
            _const_spec((D, 2 * D)),
            _const_spec((1, 2 * D)),
            _const_spec((BRANCH_WIDTH, D)),
            _const_spec((BRANCH_WIDTH, D)),
            _const_spec((D, D)),
        ],
        out_specs=pl.BlockSpec((1, tm, D), lambda b, i: (b, i, 0)),
        out_shape=jax.ShapeDtypeStruct((B, S, D), F32),
        compiler_params=pltpu.CompilerParams(
            dimension_semantics=("arbitrary", "arbitrary"),
            vmem_limit_bytes=VMEM_LIMIT),
        name="merge",
    )(x, ot, attn_g, wg, bg, wa, wb, wo)


def _ffn_kernel(x_ref, g_ref, wup_ref, cw_ref, cb_ref, wdn_ref, y_ref, gs_scr,
                *, tm, dff, chunks):
    i = pl.program_id(1)

    @pl.when(i == 0)
    def _():
        gs_scr[0:SUBLANES, :] = jnp.zeros((SUBLANES, dff), F32)

    x = x_ref[0]
    h = _rms_norm_bf16(x, g_ref[...])
    n_tiles = dff // MXU_DIM
    bounds = [MXU_DIM * ((n_tiles * c + chunks - 1) // chunks) for c in range(chunks)] + [dff]
    y = x
    for c in range(chunks):
        lo, hi = bounds[c], bounds[c + 1]
        u = jnp.dot(h, wup_ref[:, lo:hi], preferred_element_type=F32)
        g = jnp.dot(h, wup_ref[:, dff + lo:dff + hi], preferred_element_type=F32)
        gs_scr[SUBLANES:SUBLANES + tm, lo:hi] = g
        g_m1 = gs_scr[SUBLANES - 1:SUBLANES - 1 + tm, lo:hi]
        g_m2 = gs_scr[SUBLANES - 2:SUBLANES - 2 + tm, lo:hi]
        gc = (cw_ref[2:3, lo:hi] * g + cw_ref[1:2, lo:hi] * g_m1 + cw_ref[0:1, lo:hi] * g_m2
              + cb_ref[:, lo:hi])
        gs_scr[0:SUBLANES, lo:hi] = g[tm - SUBLANES:tm, :]
        act = (gc * jax.nn.sigmoid(gc) * u).astype(BF16)
        y = y + jnp.dot(act, wdn_ref[lo:hi, :], preferred_element_type=F32)
    y_ref[0] = y


def _ffn(x, ffn_g, wup, cw, cb, wdn):
    B, S, D = x.shape
    tm = FFN_TM
    dff = wdn.shape[0]
    kern = functools.partial(_ffn_kernel, tm=tm, dff=dff, chunks=FFN_CHUNKS)
    return pl.pallas_call(
        kern,
        grid=(B, S // tm),
        in_specs=[
            pl.BlockSpec((1, tm, D), lambda b, i: (b, i, 0)),
            _const_spec((1, D)),
            _const_spec((D, 2 * dff)),
            _const_spec((CONV_WIDTH, dff)),
            _const_spec((1, dff)),
            _const_spec((dff, D)),
        ],
        out_specs=pl.BlockSpec((1, tm, D), lambda b, i: (b, i, 0)),
        out_shape=jax.ShapeDtypeStruct((B, S, D), F32),
        scratch_shapes=[pltpu.VMEM((tm + SUBLANES, dff), F32)],
        compiler_params=pltpu.CompilerParams(
            dimension_semantics=("arbitrary", "arbitrary"),
            vmem_limit_bytes=VMEM_LIMIT),
        name="ffn",
    )(x, ffn_g, wup, cw, cb, wdn)


def _layer(x, attn_norm_g, w_in, b_forget, b_gate, moba_q_norm_g, moba_k_norm_g,
           fox_q_norm_g, fox_k_norm_g, w_branch_moba, w_branch_fox, w_out,
           ffn_norm_g, w_ffn_up, ffn_conv_w, ffn_conv_b, w_ffn_down):
    B, S, D = x.shape
    assert S % max(PROJ_TM, ATTN_T, MERGE_TM, FFN_TM) == 0 and S // MOBA_BLOCK <= AUG_ROWS
    nqkv = 6 * BRANCH_WIDTH
    scale = HEAD_DIM ** -0.5
    q_scale = scale * LOG2E

    w_f = jnp.pad(w_in[:, nqkv:nqkv + N_BRANCH_HEADS], ((0, 0), (0, AUG_ROWS - N_BRANCH_HEADS)))
    wt = jnp.concatenate([w_in[:, :nqkv], w_f], axis=1).T.astype(BF16)
    wg = w_in[:, nqkv + N_BRANCH_HEADS:].astype(BF16)
    gains = jnp.concatenate([moba_q_norm_g * q_scale, moba_k_norm_g,
                             fox_q_norm_g * q_scale, fox_k_norm_g])
    gains = jnp.broadcast_to(gains[:, None], (4 * HEAD_DIM, LANES)).astype(F32)
    bf = jnp.pad(b_forget.astype(F32), (0, AUG_ROWS - N_BRANCH_HEADS))
    bf = jnp.broadcast_to(bf[:, None], (AUG_ROWS, LANES))
    inv_freq = jnp.power(ROPE_THETA, -jnp.arange(ROPE_HALF, dtype=F32) * 2.0 / ROPE_DIM)
    ang = jnp.arange(S).astype(F32)[:, None] * inv_freq[None, :]
    cos_t, sin_t = jnp.cos(ang).T, jnp.sin(ang).T
    def qk_bound(gq, gk):
        return 1.01 * math.sqrt(HEAD_DIM) * jnp.max(jnp.abs(gq)) * jnp.max(jnp.abs(gk))

    thr = (F32_EXP_ZERO + 2.0 * qk_bound(fox_q_norm_g, fox_k_norm_g)).astype(F32).reshape(1)
    bound2 = LOG2E * jnp.maximum(qk_bound(moba_q_norm_g, moba_k_norm_g),
                                 qk_bound(fox_q_norm_g, fox_k_norm_g))
    bound2 = (bound2 * (1.0 + 2.0 ** -7)).astype(BF16).astype(F32)
    flag = (2.0 * bound2 <= MAX_OFFSET_SPAN).astype(jnp.int32).reshape(1)
    off = jnp.broadcast_to(-bound2, (SUBLANES, LANES)).astype(F32)

    q, k, v, cinfo = _proj(x, attn_norm_g.reshape(1, D), wt, gains, bf, off, cos_t, sin_t)
    c_first = cinfo[:, :, 0, :N_BRANCH_HEADS, 0].transpose(0, 2, 1).reshape(-1)
    c_last = cinfo[:, :, 1, :N_BRANCH_HEADS, LANES - 1].transpose(0, 2, 1).reshape(-1)
    ot = _attn(c_first, c_last, thr, flag, q, k, v)
    x1 = _merge(x, ot, attn_norm_g.reshape(1, D), wg, b_gate.reshape(1, 2 * D).astype(F32),
                w_branch_moba.astype(BF16), w_branch_fox.astype(BF16), w_out.astype(BF16))
    return _ffn(x1, ffn_norm_g.reshape(1, D), w_ffn_up.astype(BF16), ffn_conv_w.astype(F32),
                ffn_conv_b.reshape(1, -1).astype(F32), w_ffn_down.astype(BF16))


def kernel(x, attn_norm_g, w_in, b_forget, b_gate, moba_q_norm_g, moba_k_norm_g,
           fox_q_norm_g, fox_k_norm_g, w_branch_moba, w_branch_fox, w_out,
           ffn_norm_g, w_ffn_up, ffn_conv_w, ffn_conv_b, w_ffn_down):
    for l in range(attn_norm_g.shape[0]):
        x = _layer(x, attn_norm_g[l], w_in[l], b_forget[l], b_gate[l], moba_q_norm_g[l],
                   moba_k_norm_g[l], fox_q_norm_g[l], fox_k_norm_g[l], w_branch_moba[l],
                   w_branch_fox[l], w_out[l], ffn_norm_g[l], w_ffn_up[l], ffn_conv_w[l],
                   ffn_conv_b[l], w_ffn_down[l])
    return x
```

```python
import functools
import math

import jax
import jax.numpy as jnp
from jax import lax
from jax.experimental import pallas as pl
from jax.experimental.pallas import tpu as pltpu

HEAD_DIM = 64
N_BRANCH_HEADS = 8
N_HEADS = 2 * N_BRANCH_HEADS
BRANCH_WIDTH = N_BRANCH_HEADS * HEAD_DIM
MOBA_BLOCK = 256
MOBA_TOPK = 3
ROPE_THETA = 500000.0
ROPE_DIM = HEAD_DIM // 4
ROPE_HALF = ROPE_DIM // 2
CONV_WIDTH = 3
NORM_EPS = 1e-6
NEG_INF = -1e30
LOG2E = math.log2(math.e)

LANES = 128
SUBLANES = 8
MXU_DIM = 256
AUG_ROWS = 16
QK_ROWS = 128
V_ROWS = HEAD_DIM + 16
VMEM_LIMIT = 56 * 1024 * 1024

PROJ_TM = MOBA_BLOCK
PROJ_SUBTILES = 4
ATTN_T = MOBA_BLOCK
ATTN_Q_TILES = 8
MERGE_TM = 512
FFN_TM = 512
FFN_CHUNKS = 2

F32_EXP_ZERO = 104.0
MAX_OFFSET_SPAN = 100.0

F32 = jnp.float32
BF16 = jnp.bfloat16


def _split3(x):
    a = x.astype(BF16)
    r = x - a.astype(F32)
    b = r.astype(BF16)
    c = (r - b.astype(F32)).astype(BF16)
    return a, b, c


def _const_spec(shape):
    n = len(shape)
    return pl.BlockSpec(shape, lambda *_: (0,) * n, pipeline_mode=pl.Buffered(1))


def _rms_norm_bf16(x, g):
    ms = jnp.mean(x * x, axis=-1, keepdims=True)
    return (x * lax.rsqrt(ms + NORM_EPS) * g).astype(BF16)


def _proj_kernel(x_ref, g_ref, wt_ref, gain_ref, bf_ref, off_ref, cos_ref, sin_ref,
                 q_ref, k_ref, v_ref, cinfo_ref, kmean_scr, carry_scr, *, tm, nsub):
    step = pl.program_id(1)

    @pl.when(step == 0)
    def _():
        kmean_scr[...] = jnp.zeros_like(kmean_scr)
        carry_scr[...] = jnp.zeros_like(carry_scr)

    pts = []
    for sub in range(nsub):
        h = _rms_norm_bf16(x_ref[0, sub * tm:(sub + 1) * tm, :], g_ref[...])
        pts.append(lax.dot_general(wt_ref[...], h, (((1,), (1,)), ((), ())),
                                   preferred_element_type=F32))
    for sub in range(nsub):
        _proj_epilogue(pts[sub], step * nsub + sub, sub, gain_ref, bf_ref, off_ref, cos_ref, sin_ref,
                       q_ref, k_ref, v_ref, cinfo_ref, kmean_scr, carry_scr, tm=tm)


def _proj_epilogue(pt, i, sub, gain_ref, bf_ref, off_ref, cos_ref, sin_ref,
                   q_ref, k_ref, v_ref, cinfo_ref, kmean_scr, carry_scr, *, tm):
    tok = slice(sub * tm, (sub + 1) * tm)
    cos = cos_ref[:, tok]
    sin = sin_ref[:, tok]
    row16 = lax.broadcasted_iota(jnp.int32, (AUG_ROWS, tm), 0)
    pad_rows = QK_ROWS - HEAD_DIM - AUG_ROWS
    row8 = lax.broadcasted_iota(jnp.int32, (SUBLANES, tm), 0)
    pad_tail = jnp.zeros((pad_rows - SUBLANES, tm), F32)
    ones_row = jnp.where(row16 == 0, 1.0, 0.0).astype(BF16)

    def tile_lanes(a):
        return jnp.concatenate([a] * (tm // LANES), axis=1)

    q_pad = jnp.concatenate([jnp.where(row8 == 0, tile_lanes(off_ref[...]), 0.0), pad_tail], axis=0)
    k_pad = jnp.concatenate([jnp.where(row8 == 0, 1.0, 0.0), pad_tail], axis=0)

    def head_norm(t, gain):
        m = jnp.mean(t * t, axis=0, keepdims=True)
        return t * lax.rsqrt(m + NORM_EPS) * tile_lanes(gain)

    def rope(t):
        x1 = t[0:ROPE_HALF]
        x2 = t[ROPE_HALF:ROPE_DIM]
        return jnp.concatenate(
            [x1 * cos - x2 * sin, x2 * cos + x1 * sin, t[ROPE_DIM:]], axis=0)

    def k_token_major(kt, aug):
        return jnp.concatenate([kt, aug, k_pad], axis=0).T

    def emit_q(hd, qt, aug):
        q_ref[0, hd, sub, 0:HEAD_DIM, :] = qt.astype(BF16)
        q_ref[0, hd, sub, HEAD_DIM:HEAD_DIM + AUG_ROWS, :] = aug.astype(BF16)
        q_ref[0, hd, sub, HEAD_DIM + AUG_ROWS:, :] = q_pad.astype(BF16)

    def emit_k(hd, kaug):
        k_ref[0, hd, tok, :] = kaug.astype(BF16)

    def emit_v(hd, vt):
        v_ref[0, hd, sub, 0:HEAD_DIM, :] = vt.astype(BF16)
        v_ref[0, hd, sub, HEAD_DIM:, :] = ones_row

    def rows(base, hd):
        r0 = base + hd * HEAD_DIM
        return pt[r0:r0 + HEAD_DIM, :]

    gq_a = gain_ref[0 * HEAD_DIM:1 * HEAD_DIM, :]
    gk_a = gain_ref[1 * HEAD_DIM:2 * HEAD_DIM, :]
    gq_b = gain_ref[2 * HEAD_DIM:3 * HEAD_DIM, :]
    gk_b = gain_ref[3 * HEAD_DIM:4 * HEAD_DIM, :]

    own_onehot = jnp.where(row16 == i, 1.0, 0.0)
    row16f = row16.astype(F32)
    table_row = lax.broadcasted_iota(jnp.int32, (AUG_ROWS, QK_ROWS), 0)
    neg_inf = jnp.float32(-jnp.inf)
    head_lanes = lax.broadcasted_iota(jnp.int32, (1, QK_ROWS), 1) < HEAD_DIM
    zeros_head = jnp.zeros((QK_ROWS - HEAD_DIM, tm), F32)
    for hd in range(N_BRANCH_HEADS):
        qt = rope(head_norm(rows(0, hd), gq_a))
        kt = rope(head_norm(rows(BRANCH_WIDTH, hd), gk_a))
        kaug = k_token_major(kt, own_onehot)
        emit_k(hd, kaug)
        kmean = jnp.sum(kaug, axis=0, keepdims=True) * (1.0 / tm)
        kmean = jnp.where(head_lanes, kmean, 0.0)
        table = jnp.where(table_row == i, kmean, kmean_scr[hd])
        kmean_scr[hd] = table
        t_hi, t_lo, _ = _split3(table)
        q_hi, q_lo, _ = _split3(jnp.concatenate([qt, zeros_head], axis=0))
        gate = (jnp.dot(t_hi, q_hi, preferred_element_type=F32)
                + jnp.dot(t_hi, q_lo, preferred_element_type=F32)
                + jnp.dot(t_lo, q_hi, preferred_element_type=F32))
        gate = jnp.where(row16 < i, gate, neg_inf)
        sel = row16 == i
        for _ in range(MOBA_TOPK):
            mx = jnp.max(gate, axis=0, keepdims=True)
            is_mx = jnp.logical_and(gate == mx, mx > neg_inf)
            first = jnp.min(jnp.where(is_mx, row16f, float(AUG_ROWS)), axis=0, keepdims=True)
            pick = row16f == first
            sel = jnp.logical_or(sel, pick)
            gate = jnp.where(pick, neg_inf, gate)
        emit_q(hd, qt, jnp.where(sel, 0.0, NEG_INF))
        emit_v(hd, rows(2 * BRANCH_WIDTH, hd))

    z = pt[6 * BRANCH_WIDTH:6 * BRANCH_WIDTH + AUG_ROWS, :] + tile_lanes(bf_ref[...])
    logf = jnp.minimum(z, 0.0) - jnp.log1p(jnp.exp(-jnp.abs(z)))
    logf = jnp.where(row16 < N_BRANCH_HEADS, logf, 0.0)
    r_i = lax.broadcasted_iota(jnp.int32, (tm, tm), 0)
    c_i = lax.broadcasted_iota(jnp.int32, (tm, tm), 1)
    tri = jnp.where(r_i <= c_i, 1.0, 0.0).astype(BF16)
    ones_mat = jnp.ones((tm, LANES), BF16)
    parts = _split3(logf)
    c_loc = sum(jnp.dot(p, tri, preferred_element_type=F32) for p in parts)
    total = sum(jnp.dot(p, ones_mat, preferred_element_type=F32) for p in parts)
    carry = carry_scr[...]
    c = c_loc + tile_lanes(carry)
    carry_scr[...] = carry + total
    cinfo_ref[0, sub, 0] = c[:, 0:LANES]
    cinfo_ref[0, sub, 1] = c[:, tm - LANES:tm]
    c1, c2, c3 = (p.astype(F32) for p in _split3(c * LOG2E))

    def aug_rows(a, b, d, first_ones):
        o = 3 if first_ones else 0
        abd = jnp.where(row16 == o, a, jnp.where(row16 == o + 1, b, d))
        in_abd = jnp.logical_and(row16 >= o, row16 < o + 3)
        return jnp.where(in_abd, abd, jnp.where(row16 < 6, 1.0, 0.0))

    for hd in range(N_BRANCH_HEADS):
        qt = head_norm(rows(3 * BRANCH_WIDTH, hd), gq_b)
        kt = head_norm(rows(4 * BRANCH_WIDTH, hd), gk_b)
        a = jnp.broadcast_to(c1[hd:hd + 1], (AUG_ROWS, tm))
        b = jnp.broadcast_to(c2[hd:hd + 1], (AUG_ROWS, tm))
        d = jnp.broadcast_to(c3[hd:hd + 1], (AUG_ROWS, tm))
        emit_q(N_BRANCH_HEADS + hd, qt, aug_rows(a, b, d, first_ones=False))
        emit_k(N_BRANCH_HEADS + hd, k_token_major(kt, aug_rows(-a, -b, -d, first_ones=True)))
        emit_v(N_BRANCH_HEADS + hd, rows(5 * BRANCH_WIDTH, hd))


def _proj(x, attn_g, wt, gains, bf, off, cos_t, sin_t):
    B, S, D = x.shape
    tm = PROJ_TM
    nsub = PROJ_SUBTILES
    ns = S // tm
    rows = wt.shape[0]
    kern = functools.partial(_proj_kernel, tm=tm, nsub=nsub)
    return pl.pallas_call(
        kern,
        grid=(B, ns // nsub),
        in_specs=[
            pl.BlockSpec((1, nsub * tm, D), lambda b, i: (b, i, 0)),
            _const_spec((1, D)),
            _const_spec((rows, D)),
            _const_spec((4 * HEAD_DIM, LANES)),
            _const_spec((AUG_ROWS, LANES)),
            _const_spec((SUBLANES, LANES)),
            pl.BlockSpec((ROPE_HALF, nsub * tm), lambda b, i: (0, i)),
            pl.BlockSpec((ROPE_HALF, nsub * tm), lambda b, i: (0, i)),
        ],
        out_specs=[
            pl.BlockSpec((1, N_HEADS, nsub, QK_ROWS, tm), lambda b, i: (b, 0, i, 0, 0)),
            pl.BlockSpec((1, N_HEADS, nsub * tm, QK_ROWS), lambda b, i: (b, 0, i, 0)),
            pl.BlockSpec((1, N_HEADS, nsub, V_ROWS, tm), lambda b, i: (b, 0, i, 0, 0)),
            pl.BlockSpec((1, nsub, 2, AUG_ROWS, LANES), lambda b, i: (b, i, 0, 0, 0)),
        ],
        out_shape=[
            jax.ShapeDtypeStruct((B, N_HEADS, ns, QK_ROWS, tm), BF16),
            jax.ShapeDtypeStruct((B, N_HEADS, S, QK_ROWS), BF16),
            jax.ShapeDtypeStruct((B, N_HEADS, ns, V_ROWS, tm), BF16),
            jax.ShapeDtypeStruct((B, ns, 2, AUG_ROWS, LANES), F32),
        ],
        scratch_shapes=[
            pltpu.VMEM((N_BRANCH_HEADS, AUG_ROWS, QK_ROWS), F32),
            pltpu.VMEM((AUG_ROWS, LANES), F32),
        ],
        compiler_params=pltpu.CompilerParams(
            dimension_semantics=("arbitrary", "arbitrary"),
            vmem_limit_bytes=VMEM_LIMIT),
        name="proj",
    )(x, attn_g, wt, gains, bf, off, cos_t, sin_t)


def _attn_kernel(cf_ref, cl_ref, thr_ref, flag_ref, q_ref, k_ref, v_ref, o_ref, m_scr, acc_scr,
                 *, hg, t, nq, qt):
    b = pl.program_id(0)
    g = pl.program_id(1)
    step = pl.program_id(2)

    def q_tile(r, carry):
        _attn_q_tile(b, g, step * qt + r, r, cf_ref, cl_ref, thr_ref, flag_ref, q_ref, k_ref, v_ref,
                     o_ref, m_scr, acc_scr, hg=hg, t=t, nq=nq)
        return carry

    lax.fori_loop(0, qt, q_tile, 0)


def _attn_q_tile(b, g, qi, r, cf_ref, cl_ref, thr_ref, flag_ref, q_ref, k_ref, v_ref, o_ref,
                 m_scr, acc_scr, *, hg, t, nq):
    thr = thr_ref[0]
    lo = jnp.int32(nq)
    for h in range(N_BRANCH_HEADS):
        base = (b * N_BRANCH_HEADS + h) * nq
        c_q = cf_ref[base + qi]
        ok = jnp.int32(1)
        cnt = jnp.int32(0)
        for kj in range(nq):
            skip = jnp.logical_and(kj < qi, c_q - cl_ref[base + kj] < -thr)
            ok = ok * skip.astype(jnp.int32)
            cnt = cnt + ok
        lo = jnp.minimum(lo, cnt)
    lo = jnp.where(g == 1, lo, 0)

    def scores(hd, kj):
        k = k_ref[0, hd, pl.ds(pl.multiple_of(kj * t, t), t), :]
        return jnp.dot(k, q_ref[0, hd, r], preferred_element_type=F32)

    def pv_sum(hd, ps, kjs):
        pv = None
        for p, kj in zip(ps, kjs):
            d = jnp.dot(v_ref[0, hd, kj], p, preferred_element_type=F32)
            pv = d if pv is None else pv + d
        return pv

    def accumulate_bounded(hd, ss, kjs):
        ps = [jnp.exp2(s).astype(BF16) for s in ss]
        acc_scr[hd] = acc_scr[hd] + pv_sum(hd, ps, kjs)

    def accumulate_running_max(hd, ss, kjs):
        m_old = m_scr[hd]
        m_new = m_old
        for s in ss:
            m_new = jnp.maximum(m_new, jnp.max(s, axis=0, keepdims=True))
        alpha = jnp.exp2(m_old - m_new)
        ps = [jnp.exp2(s - m_new).astype(BF16) for s in ss]
        acc_scr[hd] = alpha * acc_scr[hd] + pv_sum(hd, ps, kjs)
        m_scr[hd] = m_new

    def tiles(kjs, last_mask, accumulate):
        ss = [[scores(hd, kj) for kj in kjs] for hd in range(hg)]
        for hd in range(hg):
            if last_mask is not None:
                ss[hd][-1] = jnp.where(last_mask, ss[hd][-1], -jnp.inf)
            accumulate(hd, ss[hd], kjs)

    for hd in range(hg):
        acc_scr[hd] = jnp.zeros((V_ROWS, t), F32)

    n_off = qi - lo
    key_i = lax.broadcasted_iota(jnp.int32, (t, t), 0)
    qry_i = lax.broadcasted_iota(jnp.int32, (t, t), 1)
    causal = key_i <= qry_i
    bounded = flag_ref[0] == 1

    @pl.when(bounded)
    def _():
        def body(j, carry):
            kj = lo + 2 * j
            tiles([kj, kj + 1], None, accumulate_bounded)
            return carry

        lax.fori_loop(0, lax.shift_right_logical(n_off, 1), body, 0)
        odd = (n_off & 1) == 1

        @pl.when(odd)
        def _():
            tiles([qi - 1, qi], causal, accumulate_bounded)

        @pl.when(jnp.logical_not(odd))
        def _():
            tiles([qi], causal, accumulate_bounded)

    @pl.when(jnp.logical_not(bounded))
    def _():
        for hd in range(hg):
            m_scr[hd] = jnp.full((1, t), -jnp.inf, F32)

        def body(j, carry):
            tiles([lo + j], None, accumulate_running_max)
            return carry

        lax.fori_loop(0, n_off, body, 0)
        tiles([qi], causal, accumulate_running_max)

    for hd in range(hg):
        acc = acc_scr[hd]
        o = acc[0:HEAD_DIM] / acc[HEAD_DIM:HEAD_DIM + 1]
        o_ref[0, r, hd * HEAD_DIM:(hd + 1) * HEAD_DIM, :] = o.astype(BF16)


def _attn(c_first, c_last, thr, flag, q, k, v):
    B, H, nq, _, t = q.shape
    S = nq * t
    hg = N_BRANCH_HEADS
    qt = ATTN_Q_TILES
    kern = functools.partial(_attn_kernel, hg=hg, t=t, nq=nq, qt=qt)
    grid_spec = pltpu.PrefetchScalarGridSpec(
        num_scalar_prefetch=4,
        grid=(B, H // hg, nq // qt),
        in_specs=[
            pl.BlockSpec((1, hg, qt, QK_ROWS, t), lambda b, g, i, *_: (b, g, i, 0, 0)),
            pl.BlockSpec((1, hg, S, QK_ROWS), lambda b, g, i, *_: (b, g, 0, 0)),
            pl.BlockSpec((1, hg, nq, V_ROWS, t), lambda b, g, i, *_: (b, g, 0, 0, 0)),
        ],
        out_specs=pl.BlockSpec((1, qt, hg * HEAD_DIM, t), lambda b, g, i, *_: (b, i, g, 0)),
        scratch_shapes=[
            pltpu.VMEM((hg, 1, t), F32),
            pltpu.VMEM((hg, V_ROWS, t), F32),
        ],
    )
    return pl.pallas_call(
        kern,
        grid_spec=grid_spec,
        out_shape=jax.ShapeDtypeStruct((B, nq, H * HEAD_DIM, t), BF16),
        compiler_params=pltpu.CompilerParams(
            dimension_semantics=("arbitrary", "arbitrary", "arbitrary"),
            vmem_limit_bytes=VMEM_LIMIT),
        name="attn",
    )(c_first, c_last, thr, flag, q, k, v)


def _merge_kernel(x_ref, ot_ref, g_ref, wg_ref, bg_ref, wa_ref, wb_ref, wo_ref, y_ref):
    x = x_ref[0]
    d = x.shape[-1]
    h = _rms_norm_bf16(x, g_ref[...])
    glog = jnp.dot(h, wg_ref[...], preferred_element_type=F32) + bg_ref[...]
    gates = jax.nn.sigmoid(glog)
    dn = (((0,), (0,)), ((), ()))
    ot = jnp.concatenate([ot_ref[0, j] for j in range(ot_ref.shape[1])], axis=1)
    br_a = lax.dot_general(ot[0:BRANCH_WIDTH], wa_ref[...], dn, preferred_element_type=F32)
    br_b = lax.dot_general(ot[BRANCH_WIDTH:], wb_ref[...], dn, preferred_element_type=F32)
    merged = (gates[:, :d] * br_a + gates[:, d:] * br_b).astype(BF16)
    y_ref[0] = x + jnp.dot(merged, wo_ref[...], preferred_element_type=F32)


def _merge(x, ot, attn_g, wg, bg, wa, wb, wo):
    B, S, D = x.shape
    tm = MERGE_TM
    return pl.pallas_call(
        _merge_kernel,
        grid=(B, S // tm),
        in_specs=[
            pl.BlockSpec((1, tm, D), lambda b, i: (b, i, 0)),
            pl.BlockSpec((1, tm // ATTN_T, 2 * BRANCH_WIDTH, ATTN_T), lambda b, i: (b, i, 0, 0)),
            _const_spec((1, D)),
            _const_spec((D, 2 * D)),
            _const_spec((1, 2 * D)),
            _const_spec((BRANCH_WIDTH, D)),
            _const_spec((BRANCH_WIDTH, D)),
            _const_spec((D, D)),
        ],
        out_specs=pl.BlockSpec((1, tm, D), lambda b, i: (b, i, 0)),
        out_shape=jax.ShapeDtypeStruct((B, S, D), F32),
        compiler_params=pltpu.CompilerParams(
            dimension_semantics=("arbitrary", "arbitrary"),
            vmem_limit_bytes=VMEM_LIMIT),
        name="merge",
    )(x, ot, attn_g, wg, bg, wa, wb, wo)


def _ffn_kernel(x_ref, g_ref, wup_ref, cw_ref, cb_ref, wdn_ref, y_ref, gs_scr,
                *, tm, dff, chunks):
    i = pl.program_id(1)

    @pl.when(i == 0)
    def _():
        gs_scr[0:SUBLANES, :] = jnp.zeros((SUBLANES, dff), F32)

    x = x_ref[0]
    h = _rms_norm_bf16(x, g_ref[...])
    n_tiles = dff // MXU_DIM
    bounds = [MXU_DIM * ((n_tiles * c + chunks - 1) // chunks) for c in range(chunks)] + [dff]
    y = x
    for c in range(chunks):
        lo, hi = bounds[c], bounds[c + 1]
        u = jnp.dot(h, wup_ref[:, lo:hi], preferred_element_type=F32)
        g = jnp.dot(h, wup_ref[:, dff + lo:dff + hi], preferred_element_type=F32)
        gs_scr[SUBLANES:SUBLANES + tm, lo:hi] = g
        g_m1 = gs_scr[SUBLANES - 1:SUBLANES - 1 + tm, lo:hi]
        g_m2 = gs_scr[SUBLANES - 2:SUBLANES - 2 + tm, lo:hi]
        gc = (cw_ref[2:3, lo:hi] * g + cw_ref[1:2, lo:hi] * g_m1 + cw_ref[0:1, lo:hi] * g_m2
              + cb_ref[:, lo:hi])
        gs_scr[0:SUBLANES, lo:hi] = g[tm - SUBLANES:tm, :]
        act = (gc * jax.nn.sigmoid(gc) * u).astype(BF16)
        y = y + jnp.dot(act, wdn_ref[lo:hi, :], preferred_element_type=F32)
    y_ref[0] = y


def _ffn(x, ffn_g, wup, cw, cb, wdn):
    B, S, D = x.shape
    tm = FFN_TM
    dff = wdn.shape[0]
    kern = functools.partial(_ffn_kernel, tm=tm, dff=dff, chunks=FFN_CHUNKS)
    return pl.pallas_call(
        kern,
        grid=(B, S // tm),
        in_specs=[
            pl.BlockSpec((1, tm, D), lambda b, i: (b, i, 0)),
            _const_spec((1, D)),
            _const_spec((D, 2 * dff)),
            _const_spec((CONV_WIDTH, dff)),
            _const_spec((1, dff)),
            _const_spec((dff, D)),
        ],
        out_specs=pl.BlockSpec((1, tm, D), lambda b, i: (b, i, 0)),
        out_shape=jax.ShapeDtypeStruct((B, S, D), F32),
        scratch_shapes=[pltpu.VMEM((tm + SUBLANES, dff), F32)],
        compiler_params=pltpu.CompilerParams(
            dimension_semantics=("arbitrary", "arbitrary"),
            vmem_limit_bytes=VMEM_LIMIT),
        name="ffn",
    )(x, ffn_g, wup, cw, cb, wdn)


def _layer(x, attn_norm_g, w_in, b_forget, b_gate, moba_q_norm_g, moba_k_norm_g,
           fox_q_norm_g, fox_k_norm_g, w_branch_moba, w_branch_fox, w_out,
           ffn_norm_g, w_ffn_up, ffn_conv_w, ffn_conv_b, w_ffn_down):
    B, S, D = x.shape
    assert S % max(PROJ_TM, ATTN_T, MERGE_TM, FFN_TM) == 0 and S // MOBA_BLOCK <= AUG_ROWS
    nqkv = 6 * BRANCH_WIDTH
    scale = HEAD_DIM ** -0.5
    q_scale = scale * LOG2E

    w_f = jnp.pad(w_in[:, nqkv:nqkv + N_BRANCH_HEADS], ((0, 0), (0, AUG_ROWS - N_BRANCH_HEADS)))
    wt = jnp.concatenate([w_in[:, :nqkv], w_f], axis=1).T.astype(BF16)
    wg = w_in[:, nqkv + N_BRANCH_HEADS:].astype(BF16)
    gains = jnp.concatenate([moba_q_norm_g * q_scale, moba_k_norm_g,
                             fox_q_norm_g * q_scale, fox_k_norm_g])
    gains = jnp.broadcast_to(gains[:, None], (4 * HEAD_DIM, LANES)).astype(F32)
    bf = jnp.pad(b_forget.astype(F32), (0, AUG_ROWS - N_BRANCH_HEADS))
    bf = jnp.broadcast_to(bf[:, None], (AUG_ROWS, LANES))
    inv_freq = jnp.power(ROPE_THETA, -jnp.arange(ROPE_HALF, dtype=F32) * 2.0 / ROPE_DIM)
    ang = jnp.arange(S).astype(F32)[:, None] * inv_freq[None, :]
    cos_t, sin_t = jnp.cos(ang).T, jnp.sin(ang).T
    def qk_bound(gq, gk):
        return 1.01 * math.sqrt(HEAD_DIM) * jnp.max(jnp.abs(gq)) * jnp.max(jnp.abs(gk))

    thr = (F32_EXP_ZERO + 2.0 * qk_bound(fox_q_norm_g, fox_k_norm_g)).astype(F32).reshape(1)
    bound2 = LOG2E * jnp.maximum(qk_bound(moba_q_norm_g, moba_k_norm_g),
                                 qk_bound(fox_q_norm_g, fox_k_norm_g))
    bound2 = (bound2 * (1.0 + 2.0 ** -7)).astype(BF16).astype(F32)
    flag = (2.0 * bound2 <= MAX_OFFSET_SPAN).astype(jnp.int32).reshape(1)
    off = jnp.broadcast_to(-bound2, (SUBLANES, LANES)).astype(F32)

    q, k, v, cinfo = _proj(x, attn_norm_g.reshape(1, D), wt, gains, bf, off, cos_t, sin_t)
    c_first = cinfo[:, :, 0, :N_BRANCH_HEADS, 0].transpose(0, 2, 1).reshape(-1)
    c_last = cinfo[:, :, 1, :N_BRANCH_HEADS, LANES - 1].transpose(0, 2, 1).reshape(-1)
    ot = _attn(c_first, c_last, thr, flag, q, k, v)
    x1 = _merge(x, ot, attn_norm_g.reshape(1, D), wg, b_gate.reshape(1, 2 * D).astype(F32),
                w_branch_moba.astype(BF16), w_branch_fox.astype(BF16), w_out.astype(BF16))
    return _ffn(x1, ffn_norm_g.reshape(1, D), w_ffn_up.astype(BF16), ffn_conv_w.astype(F32),
                ffn_conv_b.reshape(1, -1).astype(F32), w_ffn_down.astype(BF16))


def kernel(x, attn_norm_g, w_in, b_forget, b_gate, moba_q_norm_g, moba_k_norm_g,
           fox_q_norm_g, fox_k_norm_g, w_branch_moba, w_branch_fox, w_out,
           ffn_norm_g, w_ffn_up, ffn_conv_w, ffn_conv_b, w_ffn_down):
    for l in range(attn_norm_g.shape[0]):
        x = _layer(x, attn_norm_g[l], w_in[l], b_forget[l], b_gate[l], moba_q_norm_g[l],
                   moba_k_norm_g[l], fox_q_norm_g[l], fox_k_norm_g[l], w_branch_moba[l],
                   w_branch_fox[l], w_out[l], ffn_norm_g[l], w_ffn_up[l], ffn_conv_w[l],
                   ffn_conv_b[l], w_ffn_down[l])
    return x
```

```python
import functools
import math

import jax
import jax.numpy as jnp
from jax import lax
from jax.experimental import pallas as pl
from jax.experimental.pallas import tpu as pltpu

HEAD_DIM = 64
N_BRANCH_HEADS = 8
N_HEADS = 2 * N_BRANCH_HEADS
BRANCH_WIDTH = N_BRANCH_HEADS * HEAD_DIM
MOBA_BLOCK = 256
MOBA_TOPK = 3
ROPE_THETA = 500000.0
ROPE_DIM = HEAD_DIM // 4
ROPE_HALF = ROPE_DIM // 2
CONV_WIDTH = 3
NORM_EPS = 1e-6
NEG_INF = -1e30
LOG2E = math.log2(math.e)

LANES = 128
SUBLANES = 8
MXU_DIM = 256
AUG_ROWS = 16
QK_ROWS = 128
V_ROWS = HEAD_DIM + 16
VMEM_LIMIT = 56 * 1024 * 1024

PROJ_TM = MOBA_BLOCK
PROJ_SUBTILES = 4
ATTN_T = MOBA_BLOCK
ATTN_Q_TILES = 8
QK_LOOKAHEAD = N_BRANCH_HEADS
MERGE_TM = 512
FFN_TM = 512
FFN_CHUNKS = 2

F32_EXP_ZERO = 104.0
MAX_OFFSET_SPAN = 100.0

F32 = jnp.float32
BF16 = jnp.bfloat16


def _split3(x):
    a = x.astype(BF16)
    r = x - a.astype(F32)
    b = r.astype(BF16)
    c = (r - b.astype(F32)).astype(BF16)
    return a, b, c


def _const_spec(shape):
    n = len(shape)
    return pl.BlockSpec(shape, lambda *_: (0,) * n, pipeline_mode=pl.Buffered(1))


def _rms_norm_bf16(x, g):
    ms = jnp.mean(x * x, axis=-1, keepdims=True)
    return (x * lax.rsqrt(ms + NORM_EPS) * g).astype(BF16)


def _proj_kernel(x_ref, g_ref, wt_ref, gain_ref, bf_ref, off_ref, thr_ref, cos_ref, sin_ref,
                 q_ref, k_ref, v_ref, lo_ref, kmean_scr, carry_scr, clast_scr, *, tm, nsub):
    step = pl.program_id(1)

    @pl.when(step == 0)
    def _():
        kmean_scr[...] = jnp.zeros_like(kmean_scr)
        carry_scr[...] = jnp.zeros_like(carry_scr)
        clast_scr[...] = jnp.zeros_like(clast_scr)

    pts = []
    for sub in range(nsub):
        h = _rms_norm_bf16(x_ref[0, sub * tm:(sub + 1) * tm, :], g_ref[...])
        pts.append(lax.dot_general(wt_ref[...], h, (((1,), (1,)), ((), ())),
                                   preferred_element_type=F32))
    for sub in range(nsub):
        _proj_epilogue(pts[sub], step * nsub + sub, sub, gain_ref, bf_ref, off_ref, thr_ref, cos_ref, sin_ref,
                       q_ref, k_ref, v_ref, lo_ref, kmean_scr, carry_scr, clast_scr, tm=tm)


def _proj_epilogue(pt, i, sub, gain_ref, bf_ref, off_ref, thr_ref, cos_ref, sin_ref,
                   q_ref, k_ref, v_ref, lo_ref, kmean_scr, carry_scr, clast_scr, *, tm):
    tok = slice(sub * tm, (sub + 1) * tm)
    cos = cos_ref[:, tok]
    sin = sin_ref[:, tok]
    row16 = lax.broadcasted_iota(jnp.int32, (AUG_ROWS, tm), 0)
    pad_rows = QK_ROWS - HEAD_DIM - AUG_ROWS
    row8 = lax.broadcasted_iota(jnp.int32, (SUBLANES, tm), 0)
    pad_tail = jnp.zeros((pad_rows - SUBLANES, tm), F32)
    ones_row = jnp.where(row16 == 0, 1.0, 0.0).astype(BF16)

    def tile_lanes(a):
        return jnp.concatenate([a] * (tm // LANES), axis=1)

    q_pad = jnp.concatenate([jnp.where(row8 == 0, tile_lanes(off_ref[...]), 0.0), pad_tail], axis=0)
    k_pad = jnp.concatenate([jnp.where(row8 == 0, 1.0, 0.0), pad_tail], axis=0)

    def head_norm(t, gain):
        m = jnp.mean(t * t, axis=0, keepdims=True)
        return t * lax.rsqrt(m + NORM_EPS) * tile_lanes(gain)

    def rope(t):
        x1 = t[0:ROPE_HALF]
        x2 = t[ROPE_HALF:ROPE_DIM]
        return jnp.concatenate(
            [x1 * cos - x2 * sin, x2 * cos + x1 * sin, t[ROPE_DIM:]], axis=0)

    def k_token_major(kt, aug):
        return jnp.concatenate([kt, aug, k_pad], axis=0).T

    def emit_q(hd, qt, aug):
        q_ref[0, hd, sub, 0:HEAD_DIM, :] = qt.astype(BF16)
        q_ref[0, hd, sub, HEAD_DIM:HEAD_DIM + AUG_ROWS, :] = aug.astype(BF16)
        q_ref[0, hd, sub, HEAD_DIM + AUG_ROWS:, :] = q_pad.astype(BF16)

    def emit_k(hd, kaug):
        k_ref[0, hd, tok, :] = kaug.astype(BF16)

    def emit_v(hd, vt):
        v_ref[0, hd, sub, 0:HEAD_DIM, :] = vt.astype(BF16)
        v_ref[0, hd, sub, HEAD_DIM:, :] = ones_row

    def rows(base, hd):
        r0 = base + hd * HEAD_DIM
        return pt[r0:r0 + HEAD_DIM, :]

    gq_a = gain_ref[0 * HEAD_DIM:1 * HEAD_DIM, :]
    gk_a = gain_ref[1 * HEAD_DIM:2 * HEAD_DIM, :]
    gq_b = gain_ref[2 * HEAD_DIM:3 * HEAD_DIM, :]
    gk_b = gain_ref[3 * HEAD_DIM:4 * HEAD_DIM, :]

    own_onehot = jnp.where(row16 == i, 1.0, 0.0)
    row16f = row16.astype(F32)
    table_row = lax.broadcasted_iota(jnp.int32, (AUG_ROWS, QK_ROWS), 0)
    neg_inf = jnp.float32(-jnp.inf)
    head_lanes = lax.broadcasted_iota(jnp.int32, (1, QK_ROWS), 1) < HEAD_DIM
    zeros_head = jnp.zeros((QK_ROWS - HEAD_DIM, tm), F32)
    for hd in range(N_BRANCH_HEADS):
        qt = rope(head_norm(rows(0, hd), gq_a))
        kt = rope(head_norm(rows(BRANCH_WIDTH, hd), gk_a))
        kaug = k_token_major(kt, own_onehot)
        emit_k(hd, kaug)
        kmean = jnp.sum(kaug, axis=0, keepdims=True) * (1.0 / tm)
        kmean = jnp.where(head_lanes, kmean, 0.0)
        table = jnp.where(table_row == i, kmean, kmean_scr[hd])
        kmean_scr[hd] = table
        t_hi, t_lo, _ = _split3(table)
        q_hi, q_lo, _ = _split3(jnp.concatenate([qt, zeros_head], axis=0))
        gate = (jnp.dot(t_hi, q_hi, preferred_element_type=F32)
                + jnp.dot(t_hi, q_lo, preferred_element_type=F32)
                + jnp.dot(t_lo, q_hi, preferred_element_type=F32))
        gate = jnp.where(row16 < i, gate, neg_inf)
        sel = row16 == i
        for _ in range(MOBA_TOPK):
            mx = jnp.max(gate, axis=0, keepdims=True)
            is_mx = jnp.logical_and(gate == mx, mx > neg_inf)
            first = jnp.min(jnp.where(is_mx, row16f, float(AUG_ROWS)), axis=0, keepdims=True)
            pick = row16f == first
            sel = jnp.logical_or(sel, pick)
            gate = jnp.where(pick, neg_inf, gate)
        emit_q(hd, qt, jnp.where(sel, 0.0, NEG_INF))
        emit_v(hd, rows(2 * BRANCH_WIDTH, hd))

    z = pt[6 * BRANCH_WIDTH:6 * BRANCH_WIDTH + AUG_ROWS, :] + tile_lanes(bf_ref[...])
    logf = jnp.minimum(z, 0.0) - jnp.log1p(jnp.exp(-jnp.abs(z)))
    logf = jnp.where(row16 < N_BRANCH_HEADS, logf, 0.0)
    r_i = lax.broadcasted_iota(jnp.int32, (tm, tm), 0)
    c_i = lax.broadcasted_iota(jnp.int32, (tm, tm), 1)
    tri = jnp.where(r_i <= c_i, 1.0, 0.0).astype(BF16)
    ones_mat = jnp.ones((tm, LANES), BF16)
    parts = _split3(logf)
    c_loc = sum(jnp.dot(p, tri, preferred_element_type=F32) for p in parts)
    total = sum(jnp.dot(p, ones_mat, preferred_element_type=F32) for p in parts)
    carry = carry_scr[...]
    c = c_loc + tile_lanes(carry)
    carry_scr[...] = carry + total
    lane16 = lax.broadcasted_iota(jnp.int32, (AUG_ROWS, LANES), 1)
    c_first = jnp.broadcast_to(c[:, 0:1], (AUG_ROWS, LANES))
    c_last = jnp.broadcast_to(c[:, tm - 1:tm], (AUG_ROWS, LANES))
    hist = jnp.where(lane16 == i, c_last, clast_scr[...])
    clast_scr[...] = hist
    head_row = lax.broadcasted_iota(jnp.int32, (AUG_ROWS, LANES), 0) < N_BRANCH_HEADS
    needed = jnp.logical_and(c_first - hist >= -thr_ref[...], head_row)
    needed = jnp.logical_or(needed, lane16 >= i)
    first_needed = jnp.min(jnp.where(needed, lane16.astype(F32), float(LANES)), axis=0, keepdims=True)
    lo = jnp.min(first_needed, axis=1, keepdims=True)
    lo_ref[0, sub] = jnp.broadcast_to(lo, (SUBLANES, LANES))
    c1, c2, c3 = (p.astype(F32) for p in _split3(c * LOG2E))

    def aug_rows(a, b, d, first_ones):
        o = 3 if first_ones else 0
        abd = jnp.where(row16 == o, a, jnp.where(row16 == o + 1, b, d))
        in_abd = jnp.logical_and(row16 >= o, row16 < o + 3)
        return jnp.where(in_abd, abd, jnp.where(row16 < 6, 1.0, 0.0))

    for hd in range(N_BRANCH_HEADS):
        qt = head_norm(rows(3 * BRANCH_WIDTH, hd), gq_b)
        kt = head_norm(rows(4 * BRANCH_WIDTH, hd), gk_b)
        a = jnp.broadcast_to(c1[hd:hd + 1], (AUG_ROWS, tm))
        b = jnp.broadcast_to(c2[hd:hd + 1], (AUG_ROWS, tm))
        d = jnp.broadcast_to(c3[hd:hd + 1], (AUG_ROWS, tm))
        emit_q(N_BRANCH_HEADS + hd, qt, aug_rows(a, b, d, first_ones=False))
        emit_k(N_BRANCH_HEADS + hd, k_token_major(kt, aug_rows(-a, -b, -d, first_ones=True)))
        emit_v(N_BRANCH_HEADS + hd, rows(5 * BRANCH_WIDTH, hd))


def _proj(x, attn_g, wt, gains, bf, off, thr, cos_t, sin_t):
    B, S, D = x.shape
    tm = PROJ_TM
    nsub = PROJ_SUBTILES
    ns = S // tm
    rows = wt.shape[0]
    kern = functools.partial(_proj_kernel, tm=tm, nsub=nsub)
    return pl.pallas_call(
        kern,
        grid=(B, ns // nsub),
        in_specs=[
            pl.BlockSpec((1, nsub * tm, D), lambda b, i: (b, i, 0)),
            _const_spec((1, D)),
            _const_spec((rows, D)),
            _const_spec((4 * HEAD_DIM, LANES)),
            _const_spec((AUG_ROWS, LANES)),
            _const_spec((SUBLANES, LANES)),
            _const_spec((AUG_ROWS, LANES)),
            pl.BlockSpec((ROPE_HALF, nsub * tm), lambda b, i: (0, i)),
            pl.BlockSpec((ROPE_HALF, nsub * tm), lambda b, i: (0, i)),
        ],
        out_specs=[
            pl.BlockSpec((1, N_HEADS, nsub, QK_ROWS, tm), lambda b, i: (b, 0, i, 0, 0)),
            pl.BlockSpec((1, N_HEADS, nsub * tm, QK_ROWS), lambda b, i: (b, 0, i, 0)),
            pl.BlockSpec((1, N_HEADS, nsub, V_ROWS, tm), lambda b, i: (b, 0, i, 0, 0)),
            pl.BlockSpec((1, nsub, SUBLANES, LANES), lambda b, i: (b, i, 0, 0)),
        ],
        out_shape=[
            jax.ShapeDtypeStruct((B, N_HEADS, ns, QK_ROWS, tm), BF16),
            jax.ShapeDtypeStruct((B, N_HEADS, S, QK_ROWS), BF16),
            jax.ShapeDtypeStruct((B, N_HEADS, ns, V_ROWS, tm), BF16),
            jax.ShapeDtypeStruct((B, ns, SUBLANES, LANES), F32),
        ],
        scratch_shapes=[
            pltpu.VMEM((N_BRANCH_HEADS, AUG_ROWS, QK_ROWS), F32),
            pltpu.VMEM((AUG_ROWS, LANES), F32),
            pltpu.VMEM((AUG_ROWS, LANES), F32),
        ],
        compiler_params=pltpu.CompilerParams(
            dimension_semantics=("arbitrary", "arbitrary"),
            vmem_limit_bytes=VMEM_LIMIT),
        name="proj",
    )(x, attn_g, wt, gains, bf, off, thr, cos_t, sin_t)


def _attn_kernel(lo_ref, flag_ref, q_ref, k_ref, v_ref, o_ref, m_scr, acc_scr,
                 *, hg, t, nq, qt):
    b = pl.program_id(0)
    g = pl.program_id(1)
    step = pl.program_id(2)

    def q_tile(r, carry):
        qi = step * qt + r
        lo = jnp.where(g == 1, lo_ref[b * nq + qi], 0)
        _attn_q_tile(qi, lo, r, flag_ref, q_ref, k_ref, v_ref, o_ref, m_scr, acc_scr, hg=hg, t=t)
        return carry

    lax.fori_loop(0, qt, q_tile, 0)


def _attn_q_tile(qi, lo, r, flag_ref, q_ref, k_ref, v_ref, o_ref, m_scr, acc_scr, *, hg, t):
    def scores(hd, kjs):
        n = len(kjs)
        k = k_ref[0, hd, pl.ds(pl.multiple_of(kjs[0] * t, t), n * t), :]
        s = jnp.dot(k, q_ref[0, hd, r], preferred_element_type=F32)
        return [s[j * t:(j + 1) * t] for j in range(n)]

    def pv_sum(hd, ps, kjs):
        pv = None
        for p, kj in zip(ps, kjs):
            d = jnp.dot(v_ref[0, hd, kj], p, preferred_element_type=F32)
            pv = d if pv is None else pv + d
        return pv

    def accumulate_bounded(hd, ss, kjs):
        ps = [jnp.exp2(s).astype(BF16) for s in ss]
        acc_scr[hd] = acc_scr[hd] + pv_sum(hd, ps, kjs)

    def accumulate_running_max(hd, ss, kjs):
        m_old = m_scr[hd]
        m_new = m_old
        for s in ss:
            m_new = jnp.maximum(m_new, jnp.max(s, axis=0, keepdims=True))
        alpha = jnp.exp2(m_old - m_new)
        ps = [jnp.exp2(s - m_new).astype(BF16) for s in ss]
        acc_scr[hd] = alpha * acc_scr[hd] + pv_sum(hd, ps, kjs)
        m_scr[hd] = m_new

    def tiles(kjs, last_mask, accumulate):
        ss = {hd: scores(hd, kjs) for hd in range(min(QK_LOOKAHEAD, hg))}
        for hd in range(hg):
            nxt = hd + QK_LOOKAHEAD
            if nxt < hg:
                ss[nxt] = scores(nxt, kjs)
            s_hd = ss.pop(hd)
            if last_mask is not None:
                s_hd[-1] = jnp.where(last_mask, s_hd[-1], -jnp.inf)
            accumulate(hd, s_hd, kjs)

    for hd in range(hg):
        acc_scr[hd] = jnp.zeros((V_ROWS, t), F32)

    n_off = qi - lo
    key_i = lax.broadcasted_iota(jnp.int32, (t, t), 0)
    qry_i = lax.broadcasted_iota(jnp.int32, (t, t), 1)
    causal = key_i <= qry_i
    bounded = flag_ref[0] == 1

    @pl.when(bounded)
    def _():
        def body(j, carry):
            kj = lo + 2 * j
            tiles([kj, kj + 1], None, accumulate_bounded)
            return carry

        lax.fori_loop(0, lax.shift_right_logical(n_off, 1), body, 0)
        odd = (n_off & 1) == 1

        @pl.when(odd)
        def _():
            tiles([qi - 1, qi], causal, accumulate_bounded)

        @pl.when(jnp.logical_not(odd))
        def _():
            tiles([qi], causal, accumulate_bounded)

    @pl.when(jnp.logical_not(bounded))
    def _():
        for hd in range(hg):
            m_scr[hd] = jnp.full((1, t), -jnp.inf, F32)

        def body(j, carry):
            tiles([lo + j], None, accumulate_running_max)
            return carry

        lax.fori_loop(0, n_off, body, 0)
        tiles([qi], causal, accumulate_running_max)

    for hd in range(hg):
        acc = acc_scr[hd]
        o = acc[0:HEAD_DIM] / acc[HEAD_DIM:HEAD_DIM + 1]
        o_ref[0, r, hd * HEAD_DIM:(hd + 1) * HEAD_DIM, :] = o.astype(BF16)


def _attn(lo, flag, q, k, v):
    B, H, nq, _, t = q.shape
    S = nq * t
    hg = N_BRANCH_HEADS
    qt = ATTN_Q_TILES
    kern = functools.partial(_attn_kernel, hg=hg, t=t, nq=nq, qt=qt)
    grid_spec = pltpu.PrefetchScalarGridSpec(
        num_scalar_prefetch=2,
        grid=(B, H // hg, nq // qt),
        in_specs=[
            pl.BlockSpec((1, hg, qt, QK_ROWS, t), lambda b, g, i, *_: (b, g, i, 0, 0)),
            pl.BlockSpec((1, hg, S, QK_ROWS), lambda b, g, i, *_: (b, g, 0, 0)),
            pl.BlockSpec((1, hg, nq, V_ROWS, t), lambda b, g, i, *_: (b, g, 0, 0, 0)),
        ],
        out_specs=pl.BlockSpec((1, qt, hg * HEAD_DIM, t), lambda b, g, i, *_: (b, i, g, 0)),
        scratch_shapes=[
            pltpu.VMEM((hg, 1, t), F32),
            pltpu.VMEM((hg, V_ROWS, t), F32),
        ],
    )
    return pl.pallas_call(
        kern,
        grid_spec=grid_spec,
        out_shape=jax.ShapeDtypeStruct((B, nq, H * HEAD_DIM, t), BF16),
        compiler_params=pltpu.CompilerParams(
            dimension_semantics=("arbitrary", "arbitrary", "arbitrary"),
            vmem_limit_bytes=VMEM_LIMIT),
        name="attn",
    )(lo, flag, q, k, v)


def _merge_kernel(x_ref, ot_ref, g_ref, wg_ref, bg_ref, wa_ref, wb_ref, wo_ref, y_ref):
    x = x_ref[0]
    d = x.shape[-1]
    h = _rms_norm_bf16(x, g_ref[...])
    glog = jnp.dot(h, wg_ref[...], preferred_element_type=F32) + bg_ref[...]
    gates = jax.nn.sigmoid(glog)
    dn = (((0,), (0,)), ((), ()))
    ot = jnp.concatenate([ot_ref[0, j] for j in range(ot_ref.shape[1])], axis=1)
    br_a = lax.dot_general(ot[0:BRANCH_WIDTH], wa_ref[...], dn, preferred_element_type=F32)
    br_b = lax.dot_general(ot[BRANCH_WIDTH:], wb_ref[...], dn, preferred_element_type=F32)
    merged = (gates[:, :d] * br_a + gates[:, d:] * br_b).astype(BF16)
    y_ref[0] = x + jnp.dot(merged, wo_ref[...], preferred_element_type=F32)


def _merge(x, ot, attn_g, wg, bg, wa, wb, wo):
    B, S, D = x.shape
    tm = MERGE_TM
    return pl.pallas_call(
        _merge_kernel,
        grid=(B, S // tm),
        in_specs=[
            pl.BlockSpec((1, tm, D), lambda b, i: (b, i, 0)),
            pl.BlockSpec((1, tm // ATTN_T, 2 * BRANCH_WIDTH, ATTN_T), lambda b, i: (b, i, 0, 0)),
            _const_spec((1, D)),
            _const_spec((D, 2 * D)),
            _const_spec((1, 2 * D)),
            _const_spec((BRANCH_WIDTH, D)),
            _const_spec((BRANCH_WIDTH, D)),
            _const_spec((D, D)),
        ],
        out_specs=pl.BlockSpec((1, tm, D), lambda b, i: (b, i, 0)),
        out_shape=jax.ShapeDtypeStruct((B, S, D), F32),
        compiler_params=pltpu.CompilerParams(
            dimension_semantics=("arbitrary", "arbitrary"),
            vmem_limit_bytes=VMEM_LIMIT),
        name="merge",
    )(x, ot, attn_g, wg, bg, wa, wb, wo)


def _ffn_kernel(x_ref, g_ref, wup_ref, cw_ref, cb_ref, wdn_ref, y_ref, gs_scr,
                *, tm, dff, chunks):
    i = pl.program_id(1)

    @pl.when(i == 0)
    def _():
        gs_scr[0:SUBLANES, :] = jnp.zeros((SUBLANES, dff), F32)

    x = x_ref[0]
    h = _rms_norm_bf16(x, g_ref[...])
    n_tiles = dff // MXU_DIM
    bounds = [MXU_DIM * ((n_tiles * c + chunks - 1) // chunks) for c in range(chunks)] + [dff]
    y = x
    for c in range(chunks):
        lo, hi = bounds[c], bounds[c + 1]
        u = jnp.dot(h, wup_ref[:, lo:hi], preferred_element_type=F32)
        g = jnp.dot(h, wup_ref[:, dff + lo:dff + hi], preferred_element_type=F32)
        gs_scr[SUBLANES:SUBLANES + tm, lo:hi] = g
        g_m1 = gs_scr[SUBLANES - 1:SUBLANES - 1 + tm, lo:hi]
        g_m2 = gs_scr[SUBLANES - 2:SUBLANES - 2 + tm, lo:hi]
        gc = (cw_ref[2:3, lo:hi] * g + cw_ref[1:2, lo:hi] * g_m1 + cw_ref[0:1, lo:hi] * g_m2
              + cb_ref[:, lo:hi])
        gs_scr[0:SUBLANES, lo:hi] = g[tm - SUBLANES:tm, :]
        act = (gc * jax.nn.sigmoid(gc) * u).astype(BF16)
        y = y + jnp.dot(act, wdn_ref[lo:hi, :], preferred_element_type=F32)
    y_ref[0] = y


def _ffn(x, ffn_g, wup, cw, cb, wdn):
    B, S, D = x.shape
    tm = FFN_TM
    dff = wdn.shape[0]
    kern = functools.partial(_ffn_kernel, tm=tm, dff=dff, chunks=FFN_CHUNKS)
    return pl.pallas_call(
        kern,
        grid=(B, S // tm),
        in_specs=[
            pl.BlockSpec((1, tm, D), lambda b, i: (b, i, 0)),
            _const_spec((1, D)),
            _const_spec((D, 2 * dff)),
            _const_spec((CONV_WIDTH, dff)),
            _const_spec((1, dff)),
            _const_spec((dff, D)),
        ],
        out_specs=pl.BlockSpec((1, tm, D), lambda b, i: (b, i, 0)),
        out_shape=jax.ShapeDtypeStruct((B, S, D), F32),
        scratch_shapes=[pltpu.VMEM((tm + SUBLANES, dff), F32)],
        compiler_params=pltpu.CompilerParams(
            dimension_semantics=("arbitrary", "arbitrary"),
            vmem_limit_bytes=VMEM_LIMIT),
        name="ffn",
    )(x, ffn_g, wup, cw, cb, wdn)


def _layer(x, attn_norm_g, w_in, b_forget, b_gate, moba_q_norm_g, moba_k_norm_g,
           fox_q_norm_g, fox_k_norm_g, w_branch_moba, w_branch_fox, w_out,
           ffn_norm_g, w_ffn_up, ffn_conv_w, ffn_conv_b, w_ffn_down):
    B, S, D = x.shape
    assert S % max(PROJ_TM, ATTN_T, MERGE_TM, FFN_TM) == 0 and S // MOBA_BLOCK <= AUG_ROWS
    nqkv = 6 * BRANCH_WIDTH
    scale = HEAD_DIM ** -0.5
    q_scale = scale * LOG2E

    w_f = jnp.pad(w_in[:, nqkv:nqkv + N_BRANCH_HEADS], ((0, 0), (0, AUG_ROWS - N_BRANCH_HEADS)))
    wt = jnp.concatenate([w_in[:, :nqkv], w_f], axis=1).T.astype(BF16)
    wg = w_in[:, nqkv + N_BRANCH_HEADS:].astype(BF16)
    gains = jnp.concatenate([moba_q_norm_g * q_scale, moba_k_norm_g,
                             fox_q_norm_g * q_scale, fox_k_norm_g])
    gains = jnp.broadcast_to(gains[:, None], (4 * HEAD_DIM, LANES)).astype(F32)
    bf = jnp.pad(b_forget.astype(F32), (0, AUG_ROWS - N_BRANCH_HEADS))
    bf = jnp.broadcast_to(bf[:, None], (AUG_ROWS, LANES))
    inv_freq = jnp.power(ROPE_THETA, -jnp.arange(ROPE_HALF, dtype=F32) * 2.0 / ROPE_DIM)
    ang = jnp.arange(S).astype(F32)[:, None] * inv_freq[None, :]
    cos_t, sin_t = jnp.cos(ang).T, jnp.sin(ang).T
    def qk_bound(gq, gk):
        return 1.01 * math.sqrt(HEAD_DIM) * jnp.max(jnp.abs(gq)) * jnp.max(jnp.abs(gk))

    thr = (F32_EXP_ZERO + 2.0 * qk_bound(fox_q_norm_g, fox_k_norm_g)).astype(F32)
    thr = jnp.broadcast_to(thr, (AUG_ROWS, LANES))
    bound2 = LOG2E * jnp.maximum(qk_bound(moba_q_norm_g, moba_k_norm_g),
                                 qk_bound(fox_q_norm_g, fox_k_norm_g))
    bound2 = (bound2 * (1.0 + 2.0 ** -7)).astype(BF16).astype(F32)
    flag = (2.0 * bound2 <= MAX_OFFSET_SPAN).astype(jnp.int32).reshape(1)
    off = jnp.broadcast_to(-bound2, (SUBLANES, LANES)).astype(F32)

    q, k, v, lo = _proj(x, attn_norm_g.reshape(1, D), wt, gains, bf, off, thr, cos_t, sin_t)
    ot = _attn(lo[:, :, 0, 0].astype(jnp.int32).reshape(-1), flag, q, k, v)
    x1 = _merge(x, ot, attn_norm_g.reshape(1, D), wg, b_gate.reshape(1, 2 * D).astype(F32),
                w_branch_moba.astype(BF16), w_branch_fox.astype(BF16), w_out.astype(BF16))
    return _ffn(x1, ffn_norm_g.reshape(1, D), w_ffn_up.astype(BF16), ffn_conv_w.astype(F32),
                ffn_conv_b.reshape(1, -1).astype(F32), w_ffn_down.astype(BF16))


def kernel(x, attn_norm_g, w_in, b_forget, b_gate, moba_q_norm_g, moba_k_norm_g,
           fox_q_norm_g, fox_k_norm_g, w_branch_moba, w_branch_fox, w_out,
           ffn_norm_g, w_ffn_up, ffn_conv_w, ffn_conv_b, w_ffn_down):
    for l in range(attn_norm_g.shape[0]):
        x = _layer(x, attn_norm_g[l], w_in[l], b_forget[l], b_gate[l], moba_q_norm_g[l],
                   moba_k_norm_g[l], fox_q_norm_g[l], fox_k_norm_g[l], w_branch_moba[l],
                   w_branch_fox[l], w_out[l], ffn_norm_g[l], w_ffn_up[l], ffn_conv_w[l],
                   ffn_conv_b[l], w_ffn_down[l])
    return x
```

```python
import functools
import math

import jax
import jax.numpy as jnp
from jax import lax
from jax.experimental import pallas as pl
from jax.experimental.pallas import tpu as pltpu

HEAD_DIM = 64
N_BRANCH_HEADS = 8
N_HEADS = 2 * N_BRANCH_HEADS
BRANCH_WIDTH = N_BRANCH_HEADS * HEAD_DIM
MOBA_BLOCK = 256
MOBA_TOPK = 3
ROPE_THETA = 500000.0
ROPE_DIM = HEAD_DIM // 4
ROPE_HALF = ROPE_DIM // 2
CONV_WIDTH = 3
NORM_EPS = 1e-6
NEG_INF = -1e30
LOG2E = math.log2(math.e)

LANES = 128
SUBLANES = 8
MXU_DIM = 256
AUG_ROWS = 16
QK_ROWS = 128
V_ROWS = HEAD_DIM + 16
VMEM_LIMIT = 56 * 1024 * 1024

PROJ_TM = MOBA_BLOCK
PROJ_SUBTILES = 4
ATTN_T = MOBA_BLOCK
ATTN_Q_TILES = 8
MERGE_TM = 512
FFN_TM = 512
FFN_CHUNKS = 2

F32_EXP_ZERO = 104.0
MAX_OFFSET_SPAN = 100.0

F32 = jnp.float32
BF16 = jnp.bfloat16


def _split3(x):
    a = x.astype(BF16)
    r = x - a.astype(F32)
    b = r.astype(BF16)
    c = (r - b.astype(F32)).astype(BF16)
    return a, b, c


def _const_spec(shape):
    n = len(shape)
    return pl.BlockSpec(shape, lambda *_: (0,) * n, pipeline_mode=pl.Buffered(1))


def _rms_norm_bf16(x, g):
    ms = jnp.mean(x * x, axis=-1, keepdims=True)
    return (x * lax.rsqrt(ms + NORM_EPS) * g).astype(BF16)


def _proj_kernel(x_ref, g_ref, wt_ref, gain_ref, bf_ref, off_ref, thr_ref, cos_ref, sin_ref,
                 q_ref, k_ref, v_ref, lo_ref, kmean_scr, carry_scr, clast_scr, *, tm, nsub):
    step = pl.program_id(1)

    @pl.when(step == 0)
    def _():
        kmean_scr[...] = jnp.zeros_like(kmean_scr)
        carry_scr[...] = jnp.zeros_like(carry_scr)
        clast_scr[...] = jnp.zeros_like(clast_scr)

    pts = []
    for sub in range(nsub):
        h = _rms_norm_bf16(x_ref[0, sub * tm:(sub + 1) * tm, :], g_ref[...])
        pts.append(lax.dot_general(wt_ref[...], h, (((1,), (1,)), ((), ())),
                                   preferred_element_type=F32))
    for sub in range(nsub):
        _proj_epilogue(pts[sub], step * nsub + sub, sub, gain_ref, bf_ref, off_ref, thr_ref, cos_ref, sin_ref,
                       q_ref, k_ref, v_ref, lo_ref, kmean_scr, carry_scr, clast_scr, tm=tm)


def _proj_epilogue(pt, i, sub, gain_ref, bf_ref, off_ref, thr_ref, cos_ref, sin_ref,
                   q_ref, k_ref, v_ref, lo_ref, kmean_scr, carry_scr, clast_scr, *, tm):
    tok = slice(sub * tm, (sub + 1) * tm)
    cos = cos_ref[:, tok]
    sin = sin_ref[:, tok]
    row16 = lax.broadcasted_iota(jnp.int32, (AUG_ROWS, tm), 0)
    pad_rows = QK_ROWS - HEAD_DIM - AUG_ROWS
    row8 = lax.broadcasted_iota(jnp.int32, (SUBLANES, tm), 0)
    pad_tail = jnp.zeros((pad_rows - SUBLANES, tm), F32)
    ones_row = jnp.where(row16 == 0, 1.0, 0.0).astype(BF16)

    def tile_lanes(a):
        return jnp.concatenate([a] * (tm // LANES), axis=1)

    q_pad = jnp.concatenate([jnp.where(row8 == 0, tile_lanes(off_ref[...]), 0.0), pad_tail], axis=0)
    k_pad = jnp.concatenate([jnp.where(row8 == 0, 1.0, 0.0), pad_tail], axis=0)

    def head_norm(t, gain):
        m = jnp.mean(t * t, axis=0, keepdims=True)
        return t * lax.rsqrt(m + NORM_EPS) * tile_lanes(gain)

    def rope(t):
        x1 = t[0:ROPE_HALF]
        x2 = t[ROPE_HALF:ROPE_DIM]
        return jnp.concatenate(
            [x1 * cos - x2 * sin, x2 * cos + x1 * sin, t[ROPE_DIM:]], axis=0)

    def k_token_major(kt, aug):
        return jnp.concatenate([kt, aug, k_pad], axis=0).T

    def emit_q(hd, qt, aug):
        q_ref[0, hd, sub, 0:HEAD_DIM, :] = qt.astype(BF16)
        q_ref[0, hd, sub, HEAD_DIM:HEAD_DIM + AUG_ROWS, :] = aug.astype(BF16)
        q_ref[0, hd, sub, HEAD_DIM + AUG_ROWS:, :] = q_pad.astype(BF16)

    def emit_k(hd, kaug):
        k_ref[0, hd, tok, :] = kaug.astype(BF16)

    def emit_v(hd, vt):
        v_ref[0, hd, sub, 0:HEAD_DIM, :] = vt.astype(BF16)
        v_ref[0, hd, sub, HEAD_DIM:, :] = ones_row

    def rows(base, hd):
        r0 = base + hd * HEAD_DIM
        return pt[r0:r0 + HEAD_DIM, :]

    gq_a = gain_ref[0 * HEAD_DIM:1 * HEAD_DIM, :]
    gk_a = gain_ref[1 * HEAD_DIM:2 * HEAD_DIM, :]
    gq_b = gain_ref[2 * HEAD_DIM:3 * HEAD_DIM, :]
    gk_b = gain_ref[3 * HEAD_DIM:4 * HEAD_DIM, :]

    own_onehot = jnp.where(row16 == i, 1.0, 0.0)
    row16f = row16.astype(F32)
    table_row = lax.broadcasted_iota(jnp.int32, (AUG_ROWS, QK_ROWS), 0)
    neg_inf = jnp.float32(-jnp.inf)
    head_lanes = lax.broadcasted_iota(jnp.int32, (1, QK_ROWS), 1) < HEAD_DIM
    zeros_head = jnp.zeros((QK_ROWS - HEAD_DIM, tm), F32)
    for hd in range(N_BRANCH_HEADS):
        qt = rope(head_norm(rows(0, hd), gq_a))
        kt = rope(head_norm(rows(BRANCH_WIDTH, hd), gk_a))
        kaug = k_token_major(kt, own_onehot)
        emit_k(hd, kaug)
        kmean = jnp.sum(kaug, axis=0, keepdims=True) * (1.0 / tm)
        kmean = jnp.where(head_lanes, kmean, 0.0)
        table = jnp.where(table_row == i, kmean, kmean_scr[hd])
        kmean_scr[hd] = table
        t_hi, t_lo, _ = _split3(table)
        q_hi, q_lo, _ = _split3(jnp.concatenate([qt, zeros_head], axis=0))
        gate = (jnp.dot(t_hi, q_hi, preferred_element_type=F32)
                + jnp.dot(t_hi, q_lo, preferred_element_type=F32)
                + jnp.dot(t_lo, q_hi, preferred_element_type=F32))
        gate = jnp.where(row16 < i, gate, neg_inf)
        sel = row16 == i
        for _ in range(MOBA_TOPK):
            mx = jnp.max(gate, axis=0, keepdims=True)
            is_mx = jnp.logical_and(gate == mx, mx > neg_inf)
            first = jnp.min(jnp.where(is_mx, row16f, float(AUG_ROWS)), axis=0, keepdims=True)
            pick = row16f == first
            sel = jnp.logical_or(sel, pick)
            gate = jnp.where(pick, neg_inf, gate)
        emit_q(hd, qt, jnp.where(sel, 0.0, NEG_INF))
        emit_v(hd, rows(2 * BRANCH_WIDTH, hd))

    z = pt[6 * BRANCH_WIDTH:6 * BRANCH_WIDTH + AUG_ROWS, :] + tile_lanes(bf_ref[...])
    logf = jnp.minimum(z, 0.0) - jnp.log1p(jnp.exp(-jnp.abs(z)))
    logf = jnp.where(row16 < N_BRANCH_HEADS, logf, 0.0)
    r_i = lax.broadcasted_iota(jnp.int32, (tm, tm), 0)
    c_i = lax.broadcasted_iota(jnp.int32, (tm, tm), 1)
    tri = jnp.where(r_i <= c_i, 1.0, 0.0).astype(BF16)
    ones_mat = jnp.ones((tm, LANES), BF16)
    parts = _split3(logf)
    c_loc = sum(jnp.dot(p, tri, preferred_element_type=F32) for p in parts)
    total = sum(jnp.dot(p, ones_mat, preferred_element_type=F32) for p in parts)
    carry = carry_scr[...]
    c = c_loc + tile_lanes(carry)
    carry_scr[...] = carry + total
    lane16 = lax.broadcasted_iota(jnp.int32, (AUG_ROWS, LANES), 1)
    c_first = jnp.broadcast_to(c[:, 0:1], (AUG_ROWS, LANES))
    c_last = jnp.broadcast_to(c[:, tm - 1:tm], (AUG_ROWS, LANES))
    hist = jnp.where(lane16 == i, c_last, clast_scr[...])
    clast_scr[...] = hist
    head_row = lax.broadcasted_iota(jnp.int32, (AUG_ROWS, LANES), 0) < N_BRANCH_HEADS
    needed = jnp.logical_and(c_first - hist >= -thr_ref[...], head_row)
    needed = jnp.logical_or(needed, lane16 >= i)
    first_needed = jnp.min(jnp.where(needed, lane16.astype(F32), float(LANES)), axis=0, keepdims=True)
    lo = jnp.min(first_needed, axis=1, keepdims=True)
    lo_ref[0, sub] = jnp.broadcast_to(lo, (SUBLANES, LANES))
    c1, c2, c3 = (p.astype(F32) for p in _split3(c * LOG2E))

    def aug_rows(a, b, d, first_ones):
        o = 3 if first_ones else 0
        abd = jnp.where(row16 == o, a, jnp.where(row16 == o + 1, b, d))
        in_abd = jnp.logical_and(row16 >= o, row16 < o + 3)
        return jnp.where(in_abd, abd, jnp.where(row16 < 6, 1.0, 0.0))

    for hd in range(N_BRANCH_HEADS):
        qt = head_norm(rows(3 * BRANCH_WIDTH, hd), gq_b)
        kt = head_norm(rows(4 * BRANCH_WIDTH, hd), gk_b)
        a = jnp.broadcast_to(c1[hd:hd + 1], (AUG_ROWS, tm))
        b = jnp.broadcast_to(c2[hd:hd + 1], (AUG_ROWS, tm))
        d = jnp.broadcast_to(c3[hd:hd + 1], (AUG_ROWS, tm))
        emit_q(N_BRANCH_HEADS + hd, qt, aug_rows(a, b, d, first_ones=False))
        emit_k(N_BRANCH_HEADS + hd, k_token_major(kt, aug_rows(-a, -b, -d, first_ones=True)))
        emit_v(N_BRANCH_HEADS + hd, rows(5 * BRANCH_WIDTH, hd))


def _proj(x, attn_g, wt, gains, bf, off, thr, cos_t, sin_t):
    B, S, D = x.shape
    tm = PROJ_TM
    nsub = PROJ_SUBTILES
    ns = S // tm
    rows = wt.shape[0]
    kern = functools.partial(_proj_kernel, tm=tm, nsub=nsub)
    return pl.pallas_call(
        kern,
        grid=(B, ns // nsub),
        in_specs=[
            pl.BlockSpec((1, nsub * tm, D), lambda b, i: (b, i, 0)),
            _const_spec((1, D)),
            _const_spec((rows, D)),
            _const_spec((4 * HEAD_DIM, LANES)),
            _const_spec((AUG_ROWS, LANES)),
            _const_spec((SUBLANES, LANES)),
            _const_spec((AUG_ROWS, LANES)),
            pl.BlockSpec((ROPE_HALF, nsub * tm), lambda b, i: (0, i)),
            pl.BlockSpec((ROPE_HALF, nsub * tm), lambda b, i: (0, i)),
        ],
        out_specs=[
            pl.BlockSpec((1, N_HEADS, nsub, QK_ROWS, tm), lambda b, i: (b, 0, i, 0, 0)),
            pl.BlockSpec((1, N_HEADS, nsub * tm, QK_ROWS), lambda b, i: (b, 0, i, 0)),
            pl.BlockSpec((1, N_HEADS, nsub, V_ROWS, tm), lambda b, i: (b, 0, i, 0, 0)),
            pl.BlockSpec((1, nsub, SUBLANES, LANES), lambda b, i: (b, i, 0, 0)),
        ],
        out_shape=[
            jax.ShapeDtypeStruct((B, N_HEADS, ns, QK_ROWS, tm), BF16),
            jax.ShapeDtypeStruct((B, N_HEADS, S, QK_ROWS), BF16),
            jax.ShapeDtypeStruct((B, N_HEADS, ns, V_ROWS, tm), BF16),
            jax.ShapeDtypeStruct((B, ns, SUBLANES, LANES), F32),
        ],
        scratch_shapes=[
            pltpu.VMEM((N_BRANCH_HEADS, AUG_ROWS, QK_ROWS), F32),
            pltpu.VMEM((AUG_ROWS, LANES), F32),
            pltpu.VMEM((AUG_ROWS, LANES), F32),
        ],
        compiler_params=pltpu.CompilerParams(
            dimension_semantics=("arbitrary", "arbitrary"),
            vmem_limit_bytes=VMEM_LIMIT),
        name="proj",
    )(x, attn_g, wt, gains, bf, off, thr, cos_t, sin_t)


def _attn_kernel(lo_ref, flag_ref, q_ref, k_ref, v_ref, o_ref, m_scr, acc_scr,
                 *, hg, t, nq, qt):
    b = pl.program_id(0)
    g = pl.program_id(1)
    step = pl.program_id(2)

    def q_tile(r, carry):
        qi = step * qt + r
        lo = jnp.where(g == 1, lo_ref[b * nq + qi], 0)
        _attn_q_tile(qi, lo, r, flag_ref, q_ref, k_ref, v_ref, o_ref, m_scr, acc_scr, hg=hg, t=t)
        return carry

    lax.fori_loop(0, qt, q_tile, 0)


def _attn_q_tile(qi, lo, r, flag_ref, q_ref, k_ref, v_ref, o_ref, m_scr, acc_scr, *, hg, t):
    def scores(hd, kjs):
        n = len(kjs)
        k = k_ref[0, hd, pl.ds(pl.multiple_of(kjs[0] * t, t), n * t), :]
        s = jnp.dot(k, q_ref[0, hd, r], preferred_element_type=F32)
        return [s[j * t:(j + 1) * t] for j in range(n)]

    def pv_sum(hd, ps, kjs):
        pv = None
        for p, kj in zip(ps, kjs):
            d = jnp.dot(v_ref[0, hd, kj], p, preferred_element_type=F32)
            pv = d if pv is None else pv + d
        return pv

    def accumulate_bounded(hd, ss, kjs):
        ps = [jnp.exp2(s).astype(BF16) for s in ss]
        acc_scr[hd] = acc_scr[hd] + pv_sum(hd, ps, kjs)

    def accumulate_running_max(hd, ss, kjs):
        m_old = m_scr[hd]
        m_new = m_old
        for s in ss:
            m_new = jnp.maximum(m_new, jnp.max(s, axis=0, keepdims=True))
        alpha = jnp.exp2(m_old - m_new)
        ps = [jnp.exp2(s - m_new).astype(BF16) for s in ss]
        acc_scr[hd] = alpha * acc_scr[hd] + pv_sum(hd, ps, kjs)
        m_scr[hd] = m_new

    def tiles(kjs, last_mask, accumulate):
        ss = [scores(hd, kjs) for hd in range(hg)]
        for hd in range(hg):
            if last_mask is not None:
                ss[hd][-1] = jnp.where(last_mask, ss[hd][-1], -jnp.inf)
            accumulate(hd, ss[hd], kjs)

    for hd in range(hg):
        acc_scr[hd] = jnp.zeros((V_ROWS, t), F32)

    n_off = qi - lo
    key_i = lax.broadcasted_iota(jnp.int32, (t, t), 0)
    qry_i = lax.broadcasted_iota(jnp.int32, (t, t), 1)
    causal = key_i <= qry_i
    bounded = flag_ref[0] == 1

    @pl.when(bounded)
    def _():
        def body(j, carry):
            kj = lo + 2 * j
            tiles([kj, kj + 1], None, accumulate_bounded)
            return carry

        lax.fori_loop(0, lax.shift_right_logical(n_off, 1), body, 0)
        odd = (n_off & 1) == 1

        @pl.when(odd)
        def _():
            tiles([qi - 1, qi], causal, accumulate_bounded)

        @pl.when(jnp.logical_not(odd))
        def _():
            tiles([qi], causal, accumulate_bounded)

    @pl.when(jnp.logical_not(bounded))
    def _():
        for hd in range(hg):
            m_scr[hd] = jnp.full((1, t), -jnp.inf, F32)

        def body(j, carry):
            tiles([lo + j], None, accumulate_running_max)
            return carry

        lax.fori_loop(0, n_off, body, 0)
        tiles([qi], causal, accumulate_running_max)

    for hd in range(hg):
        acc = acc_scr[hd]
        o = acc[0:HEAD_DIM] / acc[HEAD_DIM:HEAD_DIM + 1]
        o_ref[0, r, hd * HEAD_DIM:(hd + 1) * HEAD_DIM, :] = o.astype(BF16)


def _attn(lo, flag, q, k, v):
    B, H, nq, _, t = q.shape
    S = nq * t
    hg = N_BRANCH_HEADS
    qt = ATTN_Q_TILES
    kern = functools.partial(_attn_kernel, hg=hg, t=t, nq=nq, qt=qt)
    grid_spec = pltpu.PrefetchScalarGridSpec(
        num_scalar_prefetch=2,
        grid=(B, H // hg, nq // qt),
        in_specs=[
            pl.BlockSpec((1, hg, qt, QK_ROWS, t), lambda b, g, i, *_: (b, g, i, 0, 0)),
            pl.BlockSpec((1, hg, S, QK_ROWS), lambda b, g, i, *_: (b, g, 0, 0)),
            pl.BlockSpec((1, hg, nq, V_ROWS, t), lambda b, g, i, *_: (b, g, 0, 0, 0)),
        ],
        out_specs=pl.BlockSpec((1, qt, hg * HEAD_DIM, t), lambda b, g, i, *_: (b, i, g, 0)),
        scratch_shapes=[
            pltpu.VMEM((hg, 1, t), F32),
            pltpu.VMEM((hg, V_ROWS, t), F32),
        ],
    )
    return pl.pallas_call(
        kern,
        grid_spec=grid_spec,
        out_shape=jax.ShapeDtypeStruct((B, nq, H * HEAD_DIM, t), BF16),
        compiler_params=pltpu.CompilerParams(
            dimension_semantics=("arbitrary", "arbitrary", "arbitrary"),
            vmem_limit_bytes=VMEM_LIMIT),
        name="attn",
    )(lo, flag, q, k, v)


def _merge_kernel(x_ref, ot_ref, g_ref, wg_ref, bg_ref, wa_ref, wb_ref, wo_ref, y_ref):
    x = x_ref[0]
    d = x.shape[-1]
    h = _rms_norm_bf16(x, g_ref[...])
    glog = jnp.dot(h, wg_ref[...], preferred_element_type=F32) + bg_ref[...]
    gates = jax.nn.sigmoid(glog)
    dn = (((0,), (0,)), ((), ()))
    ot = jnp.concatenate([ot_ref[0, j] for j in range(ot_ref.shape[1])], axis=1)
    br_a = lax.dot_general(ot[0:BRANCH_WIDTH], wa_ref[...], dn, preferred_element_type=F32)
    br_b = lax.dot_general(ot[BRANCH_WIDTH:], wb_ref[...], dn, preferred_element_type=F32)
    merged = (gates[:, :d] * br_a + gates[:, d:] * br_b).astype(BF16)
    y_ref[0] = x + jnp.dot(merged, wo_ref[...], preferred_element_type=F32)


def _merge(x, ot, attn_g, wg, bg, wa, wb, wo):
    B, S, D = x.shape
    tm = MERGE_TM
    return pl.pallas_call(
        _merge_kernel,
        grid=(B, S // tm),
        in_specs=[
            pl.BlockSpec((1, tm, D), lambda b, i: (b, i, 0)),
            pl.BlockSpec((1, tm // ATTN_T, 2 * BRANCH_WIDTH, ATTN_T), lambda b, i: (b, i, 0, 0)),
            _const_spec((1, D)),
            _const_spec((D, 2 * D)),
            _const_spec((1, 2 * D)),
            _const_spec((BRANCH_WIDTH, D)),
            _const_spec((BRANCH_WIDTH, D)),
            _const_spec((D, D)),
        ],
        out_specs=pl.BlockSpec((1, tm, D), lambda b, i: (b, i, 0)),
        out_shape=jax.ShapeDtypeStruct((B, S, D), F32),
        compiler_params=pltpu.CompilerParams(
            dimension_semantics=("arbitrary", "arbitrary"),
            vmem_limit_bytes=VMEM_LIMIT),
        name="merge",
    )(x, ot, attn_g, wg, bg, wa, wb, wo)


def _ffn_kernel(x_ref, g_ref, wup_ref, cw_ref, cb_ref, wdn_ref, y_ref, gs_scr,
                *, tm, dff, chunks):
    i = pl.program_id(1)

    @pl.when(i == 0)
    def _():
        gs_scr[0:SUBLANES, :] = jnp.zeros((SUBLANES, dff), F32)

    x = x_ref[0]
    h = _rms_norm_bf16(x, g_ref[...])
    n_tiles = dff // MXU_DIM
    bounds = [MXU_DIM * ((n_tiles * c + chunks - 1) // chunks) for c in range(chunks)] + [dff]
    y = x
    for c in range(chunks):
        lo, hi = bounds[c], bounds[c + 1]
        u = jnp.dot(h, wup_ref[:, lo:hi], preferred_element_type=F32)
        g = jnp.dot(h, wup_ref[:, dff + lo:dff + hi], preferred_element_type=F32)
        gs_scr[SUBLANES:SUBLANES + tm, lo:hi] = g
        g_m1 = gs_scr[SUBLANES - 1:SUBLANES - 1 + tm, lo:hi]
        g_m2 = gs_scr[SUBLANES - 2:SUBLANES - 2 + tm, lo:hi]
        gc = (cw_ref[2:3, lo:hi] * g + cw_ref[1:2, lo:hi] * g_m1 + cw_ref[0:1, lo:hi] * g_m2
              + cb_ref[:, lo:hi])
        gs_scr[0:SUBLANES, lo:hi] = g[tm - SUBLANES:tm, :]
        act = (gc * jax.nn.sigmoid(gc) * u).astype(BF16)
        y = y + jnp.dot(act, wdn_ref[lo:hi, :], preferred_element_type=F32)
    y_ref[0] = y


def _ffn(x, ffn_g, wup, cw, cb, wdn):
    B, S, D = x.shape
    tm = FFN_TM
    dff = wdn.shape[0]
    kern = functools.partial(_ffn_kernel, tm=tm, dff=dff, chunks=FFN_CHUNKS)
    return pl.pallas_call(
        kern,
        grid=(B, S // tm),
        in_specs=[
            pl.BlockSpec((1, tm, D), lambda b, i: (b, i, 0)),
            _const_spec((1, D)),
            _const_spec((D, 2 * dff)),
            _const_spec((CONV_WIDTH, dff)),
            _const_spec((1, dff)),
            _const_spec((dff, D)),
        ],
        out_specs=pl.BlockSpec((1, tm, D), lambda b, i: (b, i, 0)),
        out_shape=jax.ShapeDtypeStruct((B, S, D), F32),
        scratch_shapes=[pltpu.VMEM((tm + SUBLANES, dff), F32)],
        compiler_params=pltpu.CompilerParams(
            dimension_semantics=("arbitrary", "arbitrary"),
            vmem_limit_bytes=VMEM_LIMIT),
        name="ffn",
    )(x, ffn_g, wup, cw, cb, wdn)


def _layer(x, attn_norm_g, w_in, b_forget, b_gate, moba_q_norm_g, moba_k_norm_g,
           fox_q_norm_g, fox_k_norm_g, w_branch_moba, w_branch_fox, w_out,
           ffn_norm_g, w_ffn_up, ffn_conv_w, ffn_conv_b, w_ffn_down):
    B, S, D = x.shape
    assert S % max(PROJ_TM, ATTN_T, MERGE_TM, FFN_TM) == 0 and S // MOBA_BLOCK <= AUG_ROWS
    nqkv = 6 * BRANCH_WIDTH
    scale = HEAD_DIM ** -0.5
    q_scale = scale * LOG2E

    w_bf = w_in.astype(BF16)
    w_f = jnp.pad(w_bf[:, nqkv:nqkv + N_BRANCH_HEADS], ((0, 0), (0, AUG_ROWS - N_BRANCH_HEADS)))
    wt = jnp.concatenate([w_bf[:, :nqkv], w_f], axis=1).T
    wg = w_bf[:, nqkv + N_BRANCH_HEADS:]
    gains = jnp.concatenate([moba_q_norm_g * q_scale, moba_k_norm_g,
                             fox_q_norm_g * q_scale, fox_k_norm_g])
    gains = jnp.broadcast_to(gains[:, None], (4 * HEAD_DIM, LANES)).astype(F32)
    bf = jnp.pad(b_forget.astype(F32), (0, AUG_ROWS - N_BRANCH_HEADS))
    bf = jnp.broadcast_to(bf[:, None], (AUG_ROWS, LANES))
    inv_freq = jnp.power(ROPE_THETA, -jnp.arange(ROPE_HALF, dtype=F32) * 2.0 / ROPE_DIM)
    ang = jnp.arange(S).astype(F32)[:, None] * inv_freq[None, :]
    cos_t, sin_t = jnp.cos(ang).T, jnp.sin(ang).T
    def qk_bound(gq, gk):
        return 1.01 * math.sqrt(HEAD_DIM) * jnp.max(jnp.abs(gq)) * jnp.max(jnp.abs(gk))

    thr = (F32_EXP_ZERO + 2.0 * qk_bound(fox_q_norm_g, fox_k_norm_g)).astype(F32)
    thr = jnp.broadcast_to(thr, (AUG_ROWS, LANES))
    bound2 = LOG2E * jnp.maximum(qk_bound(moba_q_norm_g, moba_k_norm_g),
                                 qk_bound(fox_q_norm_g, fox_k_norm_g))
    bound2 = (bound2 * (1.0 + 2.0 ** -7)).astype(BF16).astype(F32)
    flag = (2.0 * bound2 <= MAX_OFFSET_SPAN).astype(jnp.int32).reshape(1)
    off = jnp.broadcast_to(-bound2, (SUBLANES, LANES)).astype(F32)

    q, k, v, lo = _proj(x, attn_norm_g.reshape(1, D), wt, gains, bf, off, thr, cos_t, sin_t)
    ot = _attn(lo[:, :, 0, 0].astype(jnp.int32).reshape(-1), flag, q, k, v)
    x1 = _merge(x, ot, attn_norm_g.reshape(1, D), wg, b_gate.reshape(1, 2 * D).astype(F32),
                w_branch_moba.astype(BF16), w_branch_fox.astype(BF16), w_out.astype(BF16))
    return _ffn(x1, ffn_norm_g.reshape(1, D), w_ffn_up.astype(BF16), ffn_conv_w.astype(F32),
                ffn_conv_b.reshape(1, -1).astype(F32), w_ffn_down.astype(BF16))


def kernel(x, attn_norm_g, w_in, b_forget, b_gate, moba_q_norm_g, moba_k_norm_g,
           fox_q_norm_g, fox_k_norm_g, w_branch_moba, w_branch_fox, w_out,
           ffn_norm_g, w_ffn_up, ffn_conv_w, ffn_conv_b, w_ffn_down):
    for l in range(attn_norm_g.shape[0]):
        x = _layer(x, attn_norm_g[l], w_in[l], b_forget[l], b_gate[l], moba_q_norm_g[l],
                   moba_k_norm_g[l], fox_q_norm_g[l], fox_k_norm_g[l], w_branch_moba[l],
                   w_branch_fox[l], w_out[l], ffn_norm_g[l], w_ffn_up[l], ffn_conv_w[l],
                   ffn_conv_b[l], w_ffn_down[l])
    return x
```

```python
import functools
import math

import jax
import jax.numpy as jnp
from jax import lax
from jax.experimental import pallas as pl
from jax.experimental.pallas import tpu as pltpu

HEAD_DIM = 64
N_BRANCH_HEADS = 8
N_HEADS = 2 * N_BRANCH_HEADS
BRANCH_WIDTH = N_BRANCH_HEADS * HEAD_DIM
MOBA_BLOCK = 256
MOBA_TOPK = 3
ROPE_THETA = 500000.0
ROPE_DIM = HEAD_DIM // 4
ROPE_HALF = ROPE_DIM // 2
CONV_WIDTH = 3
NORM_EPS = 1e-6
NEG_INF = -1e30
LOG2E = math.log2(math.e)

LANES = 128
SUBLANES = 8
MXU_DIM = 256
AUG_ROWS = 16
QK_ROWS = 128
V_ROWS = HEAD_DIM + 16
VMEM_LIMIT = 56 * 1024 * 1024

PROJ_TM = MOBA_BLOCK
PROJ_SUBTILES = 4
ATTN_T = MOBA_BLOCK
ATTN_Q_TILES = 8
MERGE_TM = 1024
FFN_TM = 512
FFN_CHUNKS = 2

F32_EXP_ZERO = 104.0
MAX_OFFSET_SPAN = 100.0

F32 = jnp.float32
BF16 = jnp.bfloat16


def _split3(x):
    a = x.astype(BF16)
    r = x - a.astype(F32)
    b = r.astype(BF16)
    c = (r - b.astype(F32)).astype(BF16)
    return a, b, c


def _const_spec(shape):
    n = len(shape)
    return pl.BlockSpec(shape, lambda *_: (0,) * n, pipeline_mode=pl.Buffered(1))


def _rms_norm_bf16(x, g):
    ms = jnp.mean(x * x, axis=-1, keepdims=True)
    return (x * lax.rsqrt(ms + NORM_EPS) * g).astype(BF16)


def _proj_kernel(x_ref, g_ref, wt_ref, gain_ref, bf_ref, off_ref, thr_ref, cos_ref, sin_ref,
                 q_ref, k_ref, v_ref, lo_ref, kmean_scr, carry_scr, clast_scr, *, tm, nsub):
    step = pl.program_id(1)

    @pl.when(step == 0)
    def _():
        kmean_scr[...] = jnp.zeros_like(kmean_scr)
        carry_scr[...] = jnp.zeros_like(carry_scr)
        clast_scr[...] = jnp.zeros_like(clast_scr)

    pts = []
    for sub in range(nsub):
        h = _rms_norm_bf16(x_ref[0, sub * tm:(sub + 1) * tm, :], g_ref[...])
        pts.append(lax.dot_general(wt_ref[...], h, (((1,), (1,)), ((), ())),
                                   preferred_element_type=F32))
    for sub in range(nsub):
        _proj_epilogue(pts[sub], step * nsub + sub, sub, gain_ref, bf_ref, off_ref, thr_ref, cos_ref, sin_ref,
                       q_ref, k_ref, v_ref, lo_ref, kmean_scr, carry_scr, clast_scr, tm=tm)


def _proj_epilogue(pt, i, sub, gain_ref, bf_ref, off_ref, thr_ref, cos_ref, sin_ref,
                   q_ref, k_ref, v_ref, lo_ref, kmean_scr, carry_scr, clast_scr, *, tm):
    tok = slice(sub * tm, (sub + 1) * tm)
    cos = cos_ref[:, tok]
    sin = sin_ref[:, tok]
    row16 = lax.broadcasted_iota(jnp.int32, (AUG_ROWS, tm), 0)
    pad_rows = QK_ROWS - HEAD_DIM - AUG_ROWS
    row8 = lax.broadcasted_iota(jnp.int32, (SUBLANES, tm), 0)
    pad_tail = jnp.zeros((pad_rows - SUBLANES, tm), F32)
    ones_row = jnp.where(row16 == 0, 1.0, 0.0).astype(BF16)

    def tile_lanes(a):
        return jnp.concatenate([a] * (tm // LANES), axis=1)

    q_pad = jnp.concatenate([jnp.where(row8 == 0, tile_lanes(off_ref[...]), 0.0), pad_tail], axis=0)
    k_pad = jnp.concatenate([jnp.where(row8 == 0, 1.0, 0.0), pad_tail], axis=0)

    def head_norm(t, gain):
        m = jnp.mean(t * t, axis=0, keepdims=True)
        return t * lax.rsqrt(m + NORM_EPS) * tile_lanes(gain)

    def rope(t):
        x1 = t[0:ROPE_HALF]
        x2 = t[ROPE_HALF:ROPE_DIM]
        return jnp.concatenate(
            [x1 * cos - x2 * sin, x2 * cos + x1 * sin, t[ROPE_DIM:]], axis=0)

    def k_token_major(kt, aug):
        return jnp.concatenate([kt, aug, k_pad], axis=0).T

    def emit_q(hd, qt, aug):
        q_ref[0, hd, sub, 0:HEAD_DIM, :] = qt.astype(BF16)
        q_ref[0, hd, sub, HEAD_DIM:HEAD_DIM + AUG_ROWS, :] = aug.astype(BF16)
        q_ref[0, hd, sub, HEAD_DIM + AUG_ROWS:, :] = q_pad.astype(BF16)

    def emit_k(hd, kaug):
        k_ref[0, hd, tok, :] = kaug.astype(BF16)

    def emit_v(hd, vt):
        v_ref[0, hd, sub, 0:HEAD_DIM, :] = vt.astype(BF16)
        v_ref[0, hd, sub, HEAD_DIM:, :] = ones_row

    def rows(base, hd):
        r0 = base + hd * HEAD_DIM
        return pt[r0:r0 + HEAD_DIM, :]

    gq_a = gain_ref[0 * HEAD_DIM:1 * HEAD_DIM, :]
    gk_a = gain_ref[1 * HEAD_DIM:2 * HEAD_DIM, :]
    gq_b = gain_ref[2 * HEAD_DIM:3 * HEAD_DIM, :]
    gk_b = gain_ref[3 * HEAD_DIM:4 * HEAD_DIM, :]

    own_onehot = jnp.where(row16 == i, 1.0, 0.0)
    row16f = row16.astype(F32)
    table_row = lax.broadcasted_iota(jnp.int32, (AUG_ROWS, QK_ROWS), 0)
    neg_inf = jnp.float32(-jnp.inf)
    head_lanes = lax.broadcasted_iota(jnp.int32, (1, QK_ROWS), 1) < HEAD_DIM
    zeros_head = jnp.zeros((QK_ROWS - HEAD_DIM, tm), F32)
    for hd in range(N_BRANCH_HEADS):
        qt = rope(head_norm(rows(0, hd), gq_a))
        kt = rope(head_norm(rows(BRANCH_WIDTH, hd), gk_a))
        kaug = k_token_major(kt, own_onehot)
        emit_k(hd, kaug)
        kmean = jnp.sum(kaug, axis=0, keepdims=True) * (1.0 / tm)
        kmean = jnp.where(head_lanes, kmean, 0.0)
        table = jnp.where(table_row == i, kmean, kmean_scr[hd])
        kmean_scr[hd] = table
        t_hi, t_lo, _ = _split3(table)
        q_hi, q_lo, _ = _split3(jnp.concatenate([qt, zeros_head], axis=0))
        gate = (jnp.dot(t_hi, q_hi, preferred_element_type=F32)
                + jnp.dot(t_hi, q_lo, preferred_element_type=F32)
                + jnp.dot(t_lo, q_hi, preferred_element_type=F32))
        gate = jnp.where(row16 < i, gate, neg_inf)
        sel = row16 == i
        for _ in range(MOBA_TOPK):
            mx = jnp.max(gate, axis=0, keepdims=True)
            is_mx = jnp.logical_and(gate == mx, mx > neg_inf)
            first = jnp.min(jnp.where(is_mx, row16f, float(AUG_ROWS)), axis=0, keepdims=True)
            pick = row16f == first
            sel = jnp.logical_or(sel, pick)
            gate = jnp.where(pick, neg_inf, gate)
        emit_q(hd, qt, jnp.where(sel, 0.0, NEG_INF))
        emit_v(hd, rows(2 * BRANCH_WIDTH, hd))

    z = pt[6 * BRANCH_WIDTH:6 * BRANCH_WIDTH + AUG_ROWS, :] + tile_lanes(bf_ref[...])
    logf = jnp.minimum(z, 0.0) - jnp.log1p(jnp.exp(-jnp.abs(z)))
    logf = jnp.where(row16 < N_BRANCH_HEADS, logf, 0.0)
    r_i = lax.broadcasted_iota(jnp.int32, (tm, tm), 0)
    c_i = lax.broadcasted_iota(jnp.int32, (tm, tm), 1)
    tri = jnp.where(r_i <= c_i, 1.0, 0.0).astype(BF16)
    ones_mat = jnp.ones((tm, LANES), BF16)
    parts = _split3(logf)
    c_loc = sum(jnp.dot(p, tri, preferred_element_type=F32) for p in parts)
    total = sum(jnp.dot(p, ones_mat, preferred_element_type=F32) for p in parts)
    carry = carry_scr[...]
    c = c_loc + tile_lanes(carry)
    carry_scr[...] = carry + total
    lane16 = lax.broadcasted_iota(jnp.int32, (AUG_ROWS, LANES), 1)
    c_first = jnp.broadcast_to(c[:, 0:1], (AUG_ROWS, LANES))
    c_last = jnp.broadcast_to(c[:, tm - 1:tm], (AUG_ROWS, LANES))
    hist = jnp.where(lane16 == i, c_last, clast_scr[...])
    clast_scr[...] = hist
    head_row = lax.broadcasted_iota(jnp.int32, (AUG_ROWS, LANES), 0) < N_BRANCH_HEADS
    needed = jnp.logical_and(c_first - hist >= -thr_ref[...], head_row)
    needed = jnp.logical_or(needed, lane16 >= i)
    first_needed = jnp.min(jnp.where(needed, lane16.astype(F32), float(LANES)), axis=0, keepdims=True)
    lo = jnp.min(first_needed, axis=1, keepdims=True)
    lo_ref[0, sub] = jnp.broadcast_to(lo, (SUBLANES, LANES))
    c1, c2, c3 = (p.astype(F32) for p in _split3(c * LOG2E))

    def aug_rows(a, b, d, first_ones):
        o = 3 if first_ones else 0
        abd = jnp.where(row16 == o, a, jnp.where(row16 == o + 1, b, d))
        in_abd = jnp.logical_and(row16 >= o, row16 < o + 3)
        return jnp.where(in_abd, abd, jnp.where(row16 < 6, 1.0, 0.0))

    for hd in range(N_BRANCH_HEADS):
        qt = head_norm(rows(3 * BRANCH_WIDTH, hd), gq_b)
        kt = head_norm(rows(4 * BRANCH_WIDTH, hd), gk_b)
        a = jnp.broadcast_to(c1[hd:hd + 1], (AUG_ROWS, tm))
        b = jnp.broadcast_to(c2[hd:hd + 1], (AUG_ROWS, tm))
        d = jnp.broadcast_to(c3[hd:hd + 1], (AUG_ROWS, tm))
        emit_q(N_BRANCH_HEADS + hd, qt, aug_rows(a, b, d, first_ones=False))
        emit_k(N_BRANCH_HEADS + hd, k_token_major(kt, aug_rows(-a, -b, -d, first_ones=True)))
        emit_v(N_BRANCH_HEADS + hd, rows(5 * BRANCH_WIDTH, hd))


def _proj(x, attn_g, wt, gains, bf, off, thr, cos_t, sin_t):
    B, S, D = x.shape
    tm = PROJ_TM
    nsub = PROJ_SUBTILES
    ns = S // tm
    rows = wt.shape[0]
    kern = functools.partial(_proj_kernel, tm=tm, nsub=nsub)
    return pl.pallas_call(
        kern,
        grid=(B, ns // nsub),
        in_specs=[
            pl.BlockSpec((1, nsub * tm, D), lambda b, i: (b, i, 0)),
            _const_spec((1, D)),
            _const_spec((rows, D)),
            _const_spec((4 * HEAD_DIM, LANES)),
            _const_spec((AUG_ROWS, LANES)),
            _const_spec((SUBLANES, LANES)),
            _const_spec((AUG_ROWS, LANES)),
            pl.BlockSpec((ROPE_HALF, nsub * tm), lambda b, i: (0, i)),
            pl.BlockSpec((ROPE_HALF, nsub * tm), lambda b, i: (0, i)),
        ],
        out_specs=[
            pl.BlockSpec((1, N_HEADS, nsub, QK_ROWS, tm), lambda b, i: (b, 0, i, 0, 0)),
            pl.BlockSpec((1, N_HEADS, nsub * tm, QK_ROWS), lambda b, i: (b, 0, i, 0)),
            pl.BlockSpec((1, N_HEADS, nsub, V_ROWS, tm), lambda b, i: (b, 0, i, 0, 0)),
            pl.BlockSpec((1, nsub, SUBLANES, LANES), lambda b, i: (b, i, 0, 0)),
        ],
        out_shape=[
            jax.ShapeDtypeStruct((B, N_HEADS, ns, QK_ROWS, tm), BF16),
            jax.ShapeDtypeStruct((B, N_HEADS, S, QK_ROWS), BF16),
            jax.ShapeDtypeStruct((B, N_HEADS, ns, V_ROWS, tm), BF16),
            jax.ShapeDtypeStruct((B, ns, SUBLANES, LANES), F32),
        ],
        scratch_shapes=[
            pltpu.VMEM((N_BRANCH_HEADS, AUG_ROWS, QK_ROWS), F32),
            pltpu.VMEM((AUG_ROWS, LANES), F32),
            pltpu.VMEM((AUG_ROWS, LANES), F32),
        ],
        compiler_params=pltpu.CompilerParams(
            dimension_semantics=("arbitrary", "arbitrary"),
            vmem_limit_bytes=VMEM_LIMIT),
        name="proj",
    )(x, attn_g, wt, gains, bf, off, thr, cos_t, sin_t)


def _attn_kernel(lo_ref, flag_ref, q_ref, k_ref, v_ref, o_ref, m_scr, acc_scr,
                 *, hg, t, nq, qt):
    b = pl.program_id(0)
    g = pl.program_id(1)
    step = pl.program_id(2)

    def q_tile(r, carry):
        qi = step * qt + r
        lo = jnp.where(g == 1, lo_ref[b * nq + qi], 0)
        _attn_q_tile(qi, lo, r, flag_ref, q_ref, k_ref, v_ref, o_ref, m_scr, acc_scr, hg=hg, t=t)
        return carry

    lax.fori_loop(0, qt, q_tile, 0)


def _attn_q_tile(qi, lo, r, flag_ref, q_ref, k_ref, v_ref, o_ref, m_scr, acc_scr, *, hg, t):
    def scores(hd, kjs):
        n = len(kjs)
        k = k_ref[0, hd, pl.ds(pl.multiple_of(kjs[0] * t, t), n * t), :]
        s = jnp.dot(k, q_ref[0, hd, r], preferred_element_type=F32)
        return [s[j * t:(j + 1) * t] for j in range(n)]

    def pv_sum(hd, ps, kjs):
        pv = None
        for p, kj in zip(ps, kjs):
            d = jnp.dot(v_ref[0, hd, kj], p, preferred_element_type=F32)
            pv = d if pv is None else pv + d
        return pv

    def accumulate_bounded(hd, ss, kjs):
        ps = [jnp.exp2(s).astype(BF16) for s in ss]
        acc_scr[hd] = acc_scr[hd] + pv_sum(hd, ps, kjs)

    def accumulate_running_max(hd, ss, kjs):
        m_old = m_scr[hd]
        m_new = m_old
        for s in ss:
            m_new = jnp.maximum(m_new, jnp.max(s, axis=0, keepdims=True))
        alpha = jnp.exp2(m_old - m_new)
        ps = [jnp.exp2(s - m_new).astype(BF16) for s in ss]
        acc_scr[hd] = alpha * acc_scr[hd] + pv_sum(hd, ps, kjs)
        m_scr[hd] = m_new

    def tiles(kjs, last_mask, accumulate):
        ss = [scores(hd, kjs) for hd in range(hg)]
        for hd in range(hg):
            if last_mask is not None:
                ss[hd][-1] = jnp.where(last_mask, ss[hd][-1], -jnp.inf)
            accumulate(hd, ss[hd], kjs)

    for hd in range(hg):
        acc_scr[hd] = jnp.zeros((V_ROWS, t), F32)

    n_off = qi - lo
    key_i = lax.broadcasted_iota(jnp.int32, (t, t), 0)
    qry_i = lax.broadcasted_iota(jnp.int32, (t, t), 1)
    causal = key_i <= qry_i
    bounded = flag_ref[0] == 1

    @pl.when(bounded)
    def _():
        def body(j, carry):
            kj = lo + 2 * j
            tiles([kj, kj + 1], None, accumulate_bounded)
            return carry

        lax.fori_loop(0, lax.shift_right_logical(n_off, 1), body, 0)
        odd = (n_off & 1) == 1

        @pl.when(odd)
        def _():
            tiles([qi - 1, qi], causal, accumulate_bounded)

        @pl.when(jnp.logical_not(odd))
        def _():
            tiles([qi], causal, accumulate_bounded)

    @pl.when(jnp.logical_not(bounded))
    def _():
        for hd in range(hg):
            m_scr[hd] = jnp.full((1, t), -jnp.inf, F32)

        def body(j, carry):
            tiles([lo + j], None, accumulate_running_max)
            return carry

        lax.fori_loop(0, n_off, body, 0)
        tiles([qi], causal, accumulate_running_max)

    for hd in range(hg):
        acc = acc_scr[hd]
        o = acc[0:HEAD_DIM] / acc[HEAD_DIM:HEAD_DIM + 1]
        o_ref[0, r, hd * HEAD_DIM:(hd + 1) * HEAD_DIM, :] = o.astype(BF16)


def _attn(lo, flag, q, k, v):
    B, H, nq, _, t = q.shape
    S = nq * t
    hg = N_BRANCH_HEADS
    qt = ATTN_Q_TILES
    kern = functools.partial(_attn_kernel, hg=hg, t=t, nq=nq, qt=qt)
    grid_spec = pltpu.PrefetchScalarGridSpec(
        num_scalar_prefetch=2,
        grid=(B, H // hg, nq // qt),
        in_specs=[
            pl.BlockSpec((1, hg, qt, QK_ROWS, t), lambda b, g, i, *_: (b, g, i, 0, 0)),
            pl.BlockSpec((1, hg, S, QK_ROWS), lambda b, g, i, *_: (b, g, 0, 0)),
            pl.BlockSpec((1, hg, nq, V_ROWS, t), lambda b, g, i, *_: (b, g, 0, 0, 0)),
        ],
        out_specs=pl.BlockSpec((1, qt, hg * HEAD_DIM, t), lambda b, g, i, *_: (b, i, g, 0)),
        scratch_shapes=[
            pltpu.VMEM((hg, 1, t), F32),
            pltpu.VMEM((hg, V_ROWS, t), F32),
        ],
    )
    return pl.pallas_call(
        kern,
        grid_spec=grid_spec,
        out_shape=jax.ShapeDtypeStruct((B, nq, H * HEAD_DIM, t), BF16),
        compiler_params=pltpu.CompilerParams(
            dimension_semantics=("arbitrary", "arbitrary", "arbitrary"),
            vmem_limit_bytes=VMEM_LIMIT),
        name="attn",
    )(lo, flag, q, k, v)


def _merge_kernel(x_ref, ot_ref, g_ref, wg_ref, bg_ref, wa_ref, wb_ref, wo_ref, y_ref):
    x = x_ref[0]
    d = x.shape[-1]
    h = _rms_norm_bf16(x, g_ref[...])
    glog = jnp.dot(h, wg_ref[...], preferred_element_type=F32) + bg_ref[...]
    gates = jax.nn.sigmoid(glog)
    dn = (((0,), (0,)), ((), ()))
    ot = jnp.concatenate([ot_ref[0, j] for j in range(ot_ref.shape[1])], axis=1)
    br_a = lax.dot_general(ot[0:BRANCH_WIDTH], wa_ref[...], dn, preferred_element_type=F32)
    br_b = lax.dot_general(ot[BRANCH_WIDTH:], wb_ref[...], dn, preferred_element_type=F32)
    merged = (gates[:, :d] * br_a + gates[:, d:] * br_b).astype(BF16)
    y_ref[0] = x + jnp.dot(merged, wo_ref[...], preferred_element_type=F32)


def _merge(x, ot, attn_g, wg, bg, wa, wb, wo):
    B, S, D = x.shape
    tm = MERGE_TM
    return pl.pallas_call(
        _merge_kernel,
        grid=(B, S // tm),
        in_specs=[
            pl.BlockSpec((1, tm, D), lambda b, i: (b, i, 0)),
            pl.BlockSpec((1, tm // ATTN_T, 2 * BRANCH_WIDTH, ATTN_T), lambda b, i: (b, i, 0, 0)),
            _const_spec((1, D)),
            _const_spec((D, 2 * D)),
            _const_spec((1, 2 * D)),
            _const_spec((BRANCH_WIDTH, D)),
            _const_spec((BRANCH_WIDTH, D)),
            _const_spec((D, D)),
        ],
        out_specs=pl.BlockSpec((1, tm, D), lambda b, i: (b, i, 0)),
        out_shape=jax.ShapeDtypeStruct((B, S, D), F32),
        compiler_params=pltpu.CompilerParams(
            dimension_semantics=("arbitrary", "arbitrary"),
            vmem_limit_bytes=VMEM_LIMIT),
        name="merge",
    )(x, ot, attn_g, wg, bg, wa, wb, wo)


def _ffn_kernel(x_ref, g_ref, wup_ref, cw_ref, cb_ref, wdn_ref, y_ref, gs_scr,
                *, tm, dff, chunks):
    i = pl.program_id(1)

    @pl.when(i == 0)
    def _():
        gs_scr[0:SUBLANES, :] = jnp.zeros((SUBLANES, dff), F32)

    x = x_ref[0]
    h = _rms_norm_bf16(x, g_ref[...])
    n_tiles = dff // MXU_DIM
    bounds = [MXU_DIM * ((n_tiles * c + chunks - 1) // chunks) for c in range(chunks)] + [dff]
    y = x
    for c in range(chunks):
        lo, hi = bounds[c], bounds[c + 1]
        u = jnp.dot(h, wup_ref[:, lo:hi], preferred_element_type=F32)
        g = jnp.dot(h, wup_ref[:, dff + lo:dff + hi], preferred_element_type=F32)
        gs_scr[SUBLANES:SUBLANES + tm, lo:hi] = g
        g_m1 = gs_scr[SUBLANES - 1:SUBLANES - 1 + tm, lo:hi]
        g_m2 = gs_scr[SUBLANES - 2:SUBLANES - 2 + tm, lo:hi]
        gc = (cw_ref[2:3, lo:hi] * g + cw_ref[1:2, lo:hi] * g_m1 + cw_ref[0:1, lo:hi] * g_m2
              + cb_ref[:, lo:hi])
        gs_scr[0:SUBLANES, lo:hi] = g[tm - SUBLANES:tm, :]
        act = (gc * jax.nn.sigmoid(gc) * u).astype(BF16)
        y = y + jnp.dot(act, wdn_ref[lo:hi, :], preferred_element_type=F32)
    y_ref[0] = y


def _ffn(x, ffn_g, wup, cw, cb, wdn):
    B, S, D = x.shape
    tm = FFN_TM
    dff = wdn.shape[0]
    kern = functools.partial(_ffn_kernel, tm=tm, dff=dff, chunks=FFN_CHUNKS)
    return pl.pallas_call(
        kern,
        grid=(B, S // tm),
        in_specs=[
            pl.BlockSpec((1, tm, D), lambda b, i: (b, i, 0)),
            _const_spec((1, D)),
            _const_spec((D, 2 * dff)),
            _const_spec((CONV_WIDTH, dff)),
            _const_spec((1, dff)),
            _const_spec((dff, D)),
        ],
        out_specs=pl.BlockSpec((1, tm, D), lambda b, i: (b, i, 0)),
        out_shape=jax.ShapeDtypeStruct((B, S, D), F32),
        scratch_shapes=[pltpu.VMEM((tm + SUBLANES, dff), F32)],
        compiler_params=pltpu.CompilerParams(
            dimension_semantics=("arbitrary", "arbitrary"),
            vmem_limit_bytes=VMEM_LIMIT),
        name="ffn",
    )(x, ffn_g, wup, cw, cb, wdn)


def _layer(x, attn_norm_g, w_in, b_forget, b_gate, moba_q_norm_g, moba_k_norm_g,
           fox_q_norm_g, fox_k_norm_g, w_branch_moba, w_branch_fox, w_out,
           ffn_norm_g, w_ffn_up, ffn_conv_w, ffn_conv_b, w_ffn_down):
    B, S, D = x.shape
    assert S % max(PROJ_TM, ATTN_T, MERGE_TM, FFN_TM) == 0 and S // MOBA_BLOCK <= AUG_ROWS
    nqkv = 6 * BRANCH_WIDTH
    scale = HEAD_DIM ** -0.5
    q_scale = scale * LOG2E

    w_bf = w_in.astype(BF16)
    w_f = jnp.pad(w_bf[:, nqkv:nqkv + N_BRANCH_HEADS], ((0, 0), (0, AUG_ROWS - N_BRANCH_HEADS)))
    wt = jnp.concatenate([w_bf[:, :nqkv], w_f], axis=1).T
    wg = w_bf[:, nqkv + N_BRANCH_HEADS:]
    gains = jnp.concatenate([moba_q_norm_g * q_scale, moba_k_norm_g,
                             fox_q_norm_g * q_scale, fox_k_norm_g])
    gains = jnp.broadcast_to(gains[:, None], (4 * HEAD_DIM, LANES)).astype(F32)
    bf = jnp.pad(b_forget.astype(F32), (0, AUG_ROWS - N_BRANCH_HEADS))
    bf = jnp.broadcast_to(bf[:, None], (AUG_ROWS, LANES))
    inv_freq = jnp.power(ROPE_THETA, -jnp.arange(ROPE_HALF, dtype=F32) * 2.0 / ROPE_DIM)
    ang = jnp.arange(S).astype(F32)[:, None] * inv_freq[None, :]
    cos_t, sin_t = jnp.cos(ang).T, jnp.sin(ang).T
    def qk_bound(gq, gk):
        return 1.01 * math.sqrt(HEAD_DIM) * jnp.max(jnp.abs(gq)) * jnp.max(jnp.abs(gk))

    thr = (F32_EXP_ZERO + 2.0 * qk_bound(fox_q_norm_g, fox_k_norm_g)).astype(F32)
    thr = jnp.broadcast_to(thr, (AUG_ROWS, LANES))
    bound2 = LOG2E * jnp.maximum(qk_bound(moba_q_norm_g, moba_k_norm_g),
                                 qk_bound(fox_q_norm_g, fox_k_norm_g))
    bound2 = (bound2 * (1.0 + 2.0 ** -7)).astype(BF16).astype(F32)
    flag = (2.0 * bound2 <= MAX_OFFSET_SPAN).astype(jnp.int32).reshape(1)
    off = jnp.broadcast_to(-bound2, (SUBLANES, LANES)).astype(F32)

    q, k, v, lo = _proj(x, attn_norm_g.reshape(1, D), wt, gains, bf, off, thr, cos_t, sin_t)
    ot = _attn(lo[:, :, 0, 0].astype(jnp.int32).reshape(-1), flag, q, k, v)
    x1 = _merge(x, ot, attn_norm_g.reshape(1, D), wg, b_gate.reshape(1, 2 * D).astype(F32),
                w_branch_moba.astype(BF16), w_branch_fox.astype(BF16), w_out.astype(BF16))
    return _ffn(x1, ffn_norm_g.reshape(1, D), w_ffn_up.astype(BF16), ffn_conv_w.astype(F32),
                ffn_conv_b.reshape(1, -1).astype(F32), w_ffn_down.astype(BF16))


def kernel(x, attn_norm_g, w_in, b_forget, b_gate, moba_q_norm_g, moba_k_norm_g,
           fox_q_norm_g, fox_k_norm_g, w_branch_moba, w_branch_fox, w_out,
           ffn_norm_g, w_ffn_up, ffn_conv_w, ffn_conv_b, w_ffn_down):
    for l in range(attn_norm_g.shape[0]):
        x = _layer(x, attn_norm_g[l], w_in[l], b_forget[l], b_gate[l], moba_q_norm_g[l],
                   moba_k_norm_g[l], fox_q_norm_g[l], fox_k_norm_g[l], w_branch_moba[l],
                   w_branch_fox[l], w_out[l], ffn_norm_g[l], w_ffn_up[l], ffn_conv_w[l],
                   ffn_conv_b[l], w_ffn_down[l])
    return x
```

```python
import functools
import math

import jax
import jax.numpy as jnp
from jax import lax
from jax.experimental import pallas as pl
from jax.experimental.pallas import tpu as pltpu

HEAD_DIM = 64
N_BRANCH_HEADS = 8
N_HEADS = 2 * N_BRANCH_HEADS
BRANCH_WIDTH = N_BRANCH_HEADS * HEAD_DIM
MOBA_BLOCK = 256
MOBA_TOPK = 3
ROPE_THETA = 500000.0
ROPE_DIM = HEAD_DIM // 4
ROPE_HALF = ROPE_DIM // 2
CONV_WIDTH = 3
NORM_EPS = 1e-6
NEG_INF = -1e30
LOG2E = math.log2(math.e)

LANES = 128
SUBLANES = 8
MXU_DIM = 256
AUG_ROWS = 16
QK_ROWS = 128
V_ROWS = HEAD_DIM + 16
VMEM_LIMIT = 56 * 1024 * 1024

PROJ_TM = MOBA_BLOCK
PROJ_SUBTILES = 4
ATTN_T = MOBA_BLOCK
ATTN_Q_TILES = 8
MERGE_TM = 1024
FFN_TM = 512
FFN_CHUNKS = 2

F32_EXP_ZERO = 104.0
MAX_OFFSET_SPAN = 100.0

F32 = jnp.float32
BF16 = jnp.bfloat16


def _split3(x):
    a = x.astype(BF16)
    r = x - a.astype(F32)
    b = r.astype(BF16)
    c = (r - b.astype(F32)).astype(BF16)
    return a, b, c


def _const_spec(shape):
    n = len(shape)
    return pl.BlockSpec(shape, lambda *_: (0,) * n, pipeline_mode=pl.Buffered(1))


def _rms_norm_bf16(x, g):
    ms = jnp.mean(x * x, axis=-1, keepdims=True)
    return (x * lax.rsqrt(ms + NORM_EPS) * g).astype(BF16)


def _proj_kernel(x_ref, g_ref, wt_ref, gain_ref, bf_ref, off_ref, thr_ref, cos_ref, sin_ref,
                 q_ref, k_ref, v_ref, lo_ref, kmean_scr, carry_scr, clast_scr, *, tm, nsub):
    step = pl.program_id(1)

    @pl.when(step == 0)
    def _():
        kmean_scr[...] = jnp.zeros_like(kmean_scr)
        carry_scr[...] = jnp.zeros_like(carry_scr)
        clast_scr[...] = jnp.zeros_like(clast_scr)

    pts = []
    for sub in range(nsub):
        h = _rms_norm_bf16(x_ref[0, sub * tm:(sub + 1) * tm, :], g_ref[...])
        pts.append(lax.dot_general(wt_ref[...], h, (((1,), (1,)), ((), ())),
                                   preferred_element_type=F32))
    for sub in range(nsub):
        _proj_epilogue(pts[sub], step * nsub + sub, sub, gain_ref, bf_ref, off_ref, thr_ref, cos_ref, sin_ref,
                       q_ref, k_ref, v_ref, lo_ref, kmean_scr, carry_scr, clast_scr, tm=tm)


def _proj_epilogue(pt, i, sub, gain_ref, bf_ref, off_ref, thr_ref, cos_ref, sin_ref,
                   q_ref, k_ref, v_ref, lo_ref, kmean_scr, carry_scr, clast_scr, *, tm):
    tok = slice(sub * tm, (sub + 1) * tm)
    cos = cos_ref[:, tok]
    sin = sin_ref[:, tok]
    row16 = lax.broadcasted_iota(jnp.int32, (AUG_ROWS, tm), 0)
    pad_rows = QK_ROWS - HEAD_DIM - AUG_ROWS
    row8 = lax.broadcasted_iota(jnp.int32, (SUBLANES, tm), 0)
    pad_tail = jnp.zeros((pad_rows - SUBLANES, tm), F32)
    ones_row = jnp.where(row16 == 0, 1.0, 0.0).astype(BF16)

    def tile_lanes(a):
        return jnp.concatenate([a] * (tm // LANES), axis=1)

    q_pad = jnp.concatenate([jnp.where(row8 == 0, tile_lanes(off_ref[...]), 0.0), pad_tail], axis=0)
    k_pad = jnp.concatenate([jnp.where(row8 == 0, 1.0, 0.0), pad_tail], axis=0)

    def head_norm(t, gain):
        m = jnp.mean(t * t, axis=0, keepdims=True)
        return t * lax.rsqrt(m + NORM_EPS) * tile_lanes(gain)

    def rope(t):
        x1 = t[0:ROPE_HALF]
        x2 = t[ROPE_HALF:ROPE_DIM]
        return jnp.concatenate(
            [x1 * cos - x2 * sin, x2 * cos + x1 * sin, t[ROPE_DIM:]], axis=0)

    def k_token_major(kt, aug):
        return jnp.concatenate([kt, aug, k_pad], axis=0).T

    def emit_q(hd, qt, aug):
        q_ref[0, hd, sub, 0:HEAD_DIM, :] = qt.astype(BF16)
        q_ref[0, hd, sub, HEAD_DIM:HEAD_DIM + AUG_ROWS, :] = aug.astype(BF16)
        q_ref[0, hd, sub, HEAD_DIM + AUG_ROWS:, :] = q_pad.astype(BF16)

    def emit_k(hd, kaug):
        k_ref[0, hd, tok, :] = kaug.astype(BF16)

    def emit_v(hd, vt):
        v_ref[0, hd, sub, 0:HEAD_DIM, :] = vt.astype(BF16)
        v_ref[0, hd, sub, HEAD_DIM:, :] = ones_row

    def rows(base, hd):
        r0 = base + hd * HEAD_DIM
        return pt[r0:r0 + HEAD_DIM, :]

    gq_a = gain_ref[0 * HEAD_DIM:1 * HEAD_DIM, :]
    gk_a = gain_ref[1 * HEAD_DIM:2 * HEAD_DIM, :]
    gq_b = gain_ref[2 * HEAD_DIM:3 * HEAD_DIM, :]
    gk_b = gain_ref[3 * HEAD_DIM:4 * HEAD_DIM, :]

    own_onehot = jnp.where(row16 == i, 1.0, 0.0)
    row16f = row16.astype(F32)
    table_row = lax.broadcasted_iota(jnp.int32, (AUG_ROWS, QK_ROWS), 0)
    neg_inf = jnp.float32(-jnp.inf)
    head_lanes = lax.broadcasted_iota(jnp.int32, (1, QK_ROWS), 1) < HEAD_DIM
    zeros_head = jnp.zeros((QK_ROWS - HEAD_DIM, tm), F32)
    for hd in range(N_BRANCH_HEADS):
        qt = rope(head_norm(rows(0, hd), gq_a))
        kt = rope(head_norm(rows(BRANCH_WIDTH, hd), gk_a))
        kaug = k_token_major(kt, own_onehot)
        emit_k(hd, kaug)
        kmean = jnp.sum(kaug, axis=0, keepdims=True) * (1.0 / tm)
        kmean = jnp.where(head_lanes, kmean, 0.0)
        table = jnp.where(table_row == i, kmean, kmean_scr[hd])
        kmean_scr[hd] = table
        t_hi, t_lo, _ = _split3(table)
        q_hi, q_lo, _ = _split3(jnp.concatenate([qt, zeros_head], axis=0))
        gate = (jnp.dot(t_hi, q_hi, preferred_element_type=F32)
                + jnp.dot(t_hi, q_lo, preferred_element_type=F32)
                + jnp.dot(t_lo, q_hi, preferred_element_type=F32))
        gate = jnp.where(row16 < i, gate, neg_inf)
        sel = row16 == i
        for _ in range(MOBA_TOPK):
            mx = jnp.max(gate, axis=0, keepdims=True)
            is_mx = jnp.logical_and(gate == mx, mx > neg_inf)
            first = jnp.min(jnp.where(is_mx, row16f, float(AUG_ROWS)), axis=0, keepdims=True)
            pick = row16f == first
            sel = jnp.logical_or(sel, pick)
            gate = jnp.where(pick, neg_inf, gate)
        emit_q(hd, qt, jnp.where(sel, 0.0, NEG_INF))
        emit_v(hd, rows(2 * BRANCH_WIDTH, hd))

    z = pt[6 * BRANCH_WIDTH:6 * BRANCH_WIDTH + AUG_ROWS, :] + tile_lanes(bf_ref[...])
    logf = jnp.minimum(z, 0.0) - jnp.log1p(jnp.exp(-jnp.abs(z)))
    logf = jnp.where(row16 < N_BRANCH_HEADS, logf, 0.0)
    r_i = lax.broadcasted_iota(jnp.int32, (tm, tm), 0)
    c_i = lax.broadcasted_iota(jnp.int32, (tm, tm), 1)
    tri = jnp.where(r_i <= c_i, 1.0, 0.0).astype(BF16)
    ones_mat = jnp.ones((tm, LANES), BF16)
    parts = _split3(logf)
    c_loc = sum(jnp.dot(p, tri, preferred_element_type=F32) for p in parts)
    total = sum(jnp.dot(p, ones_mat, preferred_element_type=F32) for p in parts)
    carry = carry_scr[...]
    c = c_loc + tile_lanes(carry)
    carry_scr[...] = carry + total
    lane16 = lax.broadcasted_iota(jnp.int32, (AUG_ROWS, LANES), 1)
    c_first = jnp.broadcast_to(c[:, 0:1], (AUG_ROWS, LANES))
    c_last = jnp.broadcast_to(c[:, tm - 1:tm], (AUG_ROWS, LANES))
    hist = jnp.where(lane16 == i, c_last, clast_scr[...])
    clast_scr[...] = hist
    head_row = lax.broadcasted_iota(jnp.int32, (AUG_ROWS, LANES), 0) < N_BRANCH_HEADS
    needed = jnp.logical_and(c_first - hist >= -thr_ref[...], head_row)
    needed = jnp.logical_or(needed, lane16 >= i)
    first_needed = jnp.min(jnp.where(needed, lane16.astype(F32), float(LANES)), axis=0, keepdims=True)
    lo = jnp.min(first_needed, axis=1, keepdims=True)
    lo_ref[0, sub] = jnp.broadcast_to(lo, (SUBLANES, LANES))
    c1, c2, c3 = (p.astype(F32) for p in _split3(c * LOG2E))

    def aug_rows(a, b, d, first_ones):
        o = 3 if first_ones else 0
        abd = jnp.where(row16 == o, a, jnp.where(row16 == o + 1, b, d))
        in_abd = jnp.logical_and(row16 >= o, row16 < o + 3)
        return jnp.where(in_abd, abd, jnp.where(row16 < 6, 1.0, 0.0))

    for hd in range(N_BRANCH_HEADS):
        qt = head_norm(rows(3 * BRANCH_WIDTH, hd), gq_b)
        kt = head_norm(rows(4 * BRANCH_WIDTH, hd), gk_b)
        a = jnp.broadcast_to(c1[hd:hd + 1], (AUG_ROWS, tm))
        b = jnp.broadcast_to(c2[hd:hd + 1], (AUG_ROWS, tm))
        d = jnp.broadcast_to(c3[hd:hd + 1], (AUG_ROWS, tm))
        emit_q(N_BRANCH_HEADS + hd, qt, aug_rows(a, b, d, first_ones=False))
        emit_k(N_BRANCH_HEADS + hd, k_token_major(kt, aug_rows(-a, -b, -d, first_ones=True)))
        emit_v(N_BRANCH_HEADS + hd, rows(5 * BRANCH_WIDTH, hd))


def _proj(x, attn_g, wt, gains, bf, off, thr, cos_t, sin_t):
    B, S, D = x.shape
    tm = PROJ_TM
    nsub = PROJ_SUBTILES
    ns = S // tm
    rows = wt.shape[0]
    kern = functools.partial(_proj_kernel, tm=tm, nsub=nsub)
    return pl.pallas_call(
        kern,
        grid=(B, ns // nsub),
        in_specs=[
            pl.BlockSpec((1, nsub * tm, D), lambda b, i: (b, i, 0)),
            _const_spec((1, D)),
            _const_spec((rows, D)),
            _const_spec((4 * HEAD_DIM, LANES)),
            _const_spec((AUG_ROWS, LANES)),
            _const_spec((SUBLANES, LANES)),
            _const_spec((AUG_ROWS, LANES)),
            pl.BlockSpec((ROPE_HALF, nsub * tm), lambda b, i: (0, i)),
            pl.BlockSpec((ROPE_HALF, nsub * tm), lambda b, i: (0, i)),
        ],
        out_specs=[
            pl.BlockSpec((1, N_HEADS, nsub, QK_ROWS, tm), lambda b, i: (b, 0, i, 0, 0)),
            pl.BlockSpec((1, N_HEADS, nsub * tm, QK_ROWS), lambda b, i: (b, 0, i, 0)),
            pl.BlockSpec((1, N_HEADS, nsub, V_ROWS, tm), lambda b, i: (b, 0, i, 0, 0)),
            pl.BlockSpec((1, nsub, SUBLANES, LANES), lambda b, i: (b, i, 0, 0)),
        ],
        out_shape=[
            jax.ShapeDtypeStruct((B, N_HEADS, ns, QK_ROWS, tm), BF16),
            jax.ShapeDtypeStruct((B, N_HEADS, S, QK_ROWS), BF16),
            jax.ShapeDtypeStruct((B, N_HEADS, ns, V_ROWS, tm), BF16),
            jax.ShapeDtypeStruct((B, ns, SUBLANES, LANES), F32),
        ],
        scratch_shapes=[
            pltpu.VMEM((N_BRANCH_HEADS, AUG_ROWS, QK_ROWS), F32),
            pltpu.VMEM((AUG_ROWS, LANES), F32),
            pltpu.VMEM((AUG_ROWS, LANES), F32),
        ],
        compiler_params=pltpu.CompilerParams(
            dimension_semantics=("arbitrary", "arbitrary"),
            vmem_limit_bytes=VMEM_LIMIT),
        name="proj",
    )(x, attn_g, wt, gains, bf, off, thr, cos_t, sin_t)


def _attn_kernel(lo_ref, flag_ref, q_ref, k_ref, v_ref, o_ref, m_scr, acc_scr,
                 *, hg, t, nq, qt):
    b = pl.program_id(0)
    g = pl.program_id(1)
    step = pl.program_id(2)

    def q_tile(r, carry):
        qi = step * qt + r
        lo = jnp.where(g == 1, lo_ref[b * nq + qi], 0)
        _attn_q_tile(qi, lo, r, flag_ref, q_ref, k_ref, v_ref, o_ref, m_scr, acc_scr, hg=hg, t=t)
        return carry

    lax.fori_loop(0, qt, q_tile, 0)


def _attn_q_tile(qi, lo, r, flag_ref, q_ref, k_ref, v_ref, o_ref, m_scr, acc_scr, *, hg, t):
    def scores(hd, kjs):
        n = len(kjs)
        k = k_ref[0, hd, pl.ds(pl.multiple_of(kjs[0] * t, t), n * t), :]
        s = jnp.dot(k, q_ref[0, hd, r], preferred_element_type=F32)
        return [s[j * t:(j + 1) * t] for j in range(n)]

    def pv_sum(hd, ps, kjs):
        pv = None
        for p, kj in zip(ps, kjs):
            d = jnp.dot(v_ref[0, hd, kj], p, preferred_element_type=F32)
            pv = d if pv is None else pv + d
        return pv

    def accumulate_bounded(hd, ss, kjs):
        ps = [jnp.exp2(s).astype(BF16) for s in ss]
        acc_scr[hd] = acc_scr[hd] + pv_sum(hd, ps, kjs)

    def accumulate_running_max(hd, ss, kjs):
        m_old = m_scr[hd]
        m_new = m_old
        for s in ss:
            m_new = jnp.maximum(m_new, jnp.max(s, axis=0, keepdims=True))
        alpha = jnp.exp2(m_old - m_new)
        ps = [jnp.exp2(s - m_new).astype(BF16) for s in ss]
        acc_scr[hd] = alpha * acc_scr[hd] + pv_sum(hd, ps, kjs)
        m_scr[hd] = m_new

    def tiles(kjs, last_mask, accumulate):
        ss = [scores(hd, kjs) for hd in range(hg)]
        for hd in range(hg):
            if last_mask is not None:
                ss[hd][-1] = jnp.where(last_mask, ss[hd][-1], -jnp.inf)
            accumulate(hd, ss[hd], kjs)

    for hd in range(hg):
        acc_scr[hd] = jnp.zeros((V_ROWS, t), F32)

    n_off = qi - lo
    key_i = lax.broadcasted_iota(jnp.int32, (t, t), 0)
    qry_i = lax.broadcasted_iota(jnp.int32, (t, t), 1)
    causal = key_i <= qry_i
    bounded = flag_ref[0] == 1

    @pl.when(bounded)
    def _():
        def body(j, carry):
            kj = lo + 2 * j
            tiles([kj, kj + 1], None, accumulate_bounded)
            return carry

        lax.fori_loop(0, lax.shift_right_logical(n_off, 1), body, 0)
        odd = (n_off & 1) == 1

        @pl.when(odd)
        def _():
            tiles([qi - 1, qi], causal, accumulate_bounded)

        @pl.when(jnp.logical_not(odd))
        def _():
            tiles([qi], causal, accumulate_bounded)

    @pl.when(jnp.logical_not(bounded))
    def _():
        for hd in range(hg):
            m_scr[hd] = jnp.full((1, t), -jnp.inf, F32)

        def body(j, carry):
            tiles([lo + j], None, accumulate_running_max)
            return carry

        lax.fori_loop(0, n_off, body, 0)
        tiles([qi], causal, accumulate_running_max)

    for hd in range(hg):
        acc = acc_scr[hd]
        o = acc[0:HEAD_DIM] / acc[HEAD_DIM:HEAD_DIM + 1]
        o_ref[0, r, hd * HEAD_DIM:(hd + 1) * HEAD_DIM, :] = o.astype(BF16)


def _attn(lo, flag, q, k, v):
    B, H, nq, _, t = q.shape
    S = nq * t
    hg = N_BRANCH_HEADS
    qt = ATTN_Q_TILES
    kern = functools.partial(_attn_kernel, hg=hg, t=t, nq=nq, qt=qt)
    grid_spec = pltpu.PrefetchScalarGridSpec(
        num_scalar_prefetch=2,
        grid=(B, H // hg, nq // qt),
        in_specs=[
            pl.BlockSpec((1, hg, qt, QK_ROWS, t), lambda b, g, i, *_: (b, g, i, 0, 0)),
            pl.BlockSpec((1, hg, S, QK_ROWS), lambda b, g, i, *_: (b, g, 0, 0)),
            pl.BlockSpec((1, hg, nq, V_ROWS, t), lambda b, g, i, *_: (b, g, 0, 0, 0)),
        ],
        out_specs=pl.BlockSpec((1, qt, hg * HEAD_DIM, t), lambda b, g, i, *_: (b, i, g, 0)),
        scratch_shapes=[
            pltpu.VMEM((hg, 1, t), F32),
            pltpu.VMEM((hg, V_ROWS, t), F32),
        ],
    )
    return pl.pallas_call(
        kern,
        grid_spec=grid_spec,
        out_shape=jax.ShapeDtypeStruct((B, nq, H * HEAD_DIM, t), BF16),
        compiler_params=pltpu.CompilerParams(
            dimension_semantics=("arbitrary", "arbitrary", "arbitrary"),
            vmem_limit_bytes=VMEM_LIMIT),
        name="attn",
    )(lo, flag, q, k, v)


def _merge_kernel(x_ref, ot_ref, g_ref, wg_ref, bg_ref, wa_ref, wb_ref, wo_ref, y_ref):
    x = x_ref[0]
    d = x.shape[-1]
    h = _rms_norm_bf16(x, g_ref[...])
    glog = jnp.dot(h, wg_ref[...], preferred_element_type=F32) + bg_ref[...]
    gates = jax.nn.sigmoid(glog)
    dn = (((0,), (0,)), ((), ()))
    ot = jnp.concatenate([ot_ref[0, j] for j in range(ot_ref.shape[1])], axis=1)
    br_a = lax.dot_general(ot[0:BRANCH_WIDTH], wa_ref[...], dn, preferred_element_type=F32)
    br_b = lax.dot_general(ot[BRANCH_WIDTH:], wb_ref[...], dn, preferred_element_type=F32)
    merged = (gates[:, :d] * br_a + gates[:, d:] * br_b).astype(BF16)
    y_ref[0] = x + jnp.dot(merged, wo_ref[...], preferred_element_type=F32)


def _merge(x, ot, attn_g, wg, bg, wa, wb, wo):
    B, S, D = x.shape
    tm = MERGE_TM
    return pl.pallas_call(
        _merge_kernel,
        grid=(B, S // tm),
        in_specs=[
            pl.BlockSpec((1, tm, D), lambda b, i: (b, i, 0)),
            pl.BlockSpec((1, tm // ATTN_T, 2 * BRANCH_WIDTH, ATTN_T), lambda b, i: (b, i, 0, 0)),
            _const_spec((1, D)),
            _const_spec((D, 2 * D)),
            _const_spec((1, 2 * D)),
            _const_spec((BRANCH_WIDTH, D)),
            _const_spec((BRANCH_WIDTH, D)),
            _const_spec((D, D)),
        ],
        out_specs=pl.BlockSpec((1, tm, D), lambda b, i: (b, i, 0)),
        out_shape=jax.ShapeDtypeStruct((B, S, D), F32),
        compiler_params=pltpu.CompilerParams(
            dimension_semantics=("arbitrary", "arbitrary"),
            vmem_limit_bytes=VMEM_LIMIT),
        name="merge",
    )(x, ot, attn_g, wg, bg, wa, wb, wo)


def _ffn_kernel(x_ref, g_ref, wup_ref, cw_ref, cb_ref, wdn_ref, y_ref, gs_scr,
                *, tm, dff, chunks):
    i = pl.program_id(1)

    @pl.when(i == 0)
    def _():
        gs_scr[0:SUBLANES, :] = jnp.zeros((SUBLANES, dff), F32)

    x = x_ref[0]
    h = _rms_norm_bf16(x, g_ref[...])
    n_tiles = dff // MXU_DIM
    bounds = [MXU_DIM * ((n_tiles * c + chunks - 1) // chunks) for c in range(chunks)] + [dff]
    y = x
    gs = [jnp.dot(h, wup_ref[:, dff + bounds[c]:dff + bounds[c + 1]], preferred_element_type=F32)
          for c in range(chunks)]
    us = [jnp.dot(h, wup_ref[:, bounds[c]:bounds[c + 1]], preferred_element_type=F32)
          for c in range(chunks)]
    for c in range(chunks):
        lo, hi = bounds[c], bounds[c + 1]
        u = us[c]
        g = gs[c]
        gs_scr[SUBLANES:SUBLANES + tm, lo:hi] = g
        g_m1 = gs_scr[SUBLANES - 1:SUBLANES - 1 + tm, lo:hi]
        g_m2 = gs_scr[SUBLANES - 2:SUBLANES - 2 + tm, lo:hi]
        gc = (cw_ref[2:3, lo:hi] * g + cw_ref[1:2, lo:hi] * g_m1 + cw_ref[0:1, lo:hi] * g_m2
              + cb_ref[:, lo:hi])
        gs_scr[0:SUBLANES, lo:hi] = g[tm - SUBLANES:tm, :]
        act = (gc * jax.nn.sigmoid(gc) * u).astype(BF16)
        y = y + jnp.dot(act, wdn_ref[lo:hi, :], preferred_element_type=F32)
    y_ref[0] = y


def _ffn(x, ffn_g, wup, cw, cb, wdn):
    B, S, D = x.shape
    tm = FFN_TM
    dff = wdn.shape[0]
    kern = functools.partial(_ffn_kernel, tm=tm, dff=dff, chunks=FFN_CHUNKS)
    return pl.pallas_call(
        kern,
        grid=(B, S // tm),
        in_specs=[
            pl.BlockSpec((1, tm, D), lambda b, i: (b, i, 0)),
            _const_spec((1, D)),
            _const_spec((D, 2 * dff)),
            _const_spec((CONV_WIDTH, dff)),
            _const_spec((1, dff)),
            _const_spec((dff, D)),
        ],
        out_specs=pl.BlockSpec((1, tm, D), lambda b, i: (b, i, 0)),
        out_shape=jax.ShapeDtypeStruct((B, S, D), F32),
        scratch_shapes=[pltpu.VMEM((tm + SUBLANES, dff), F32)],
        compiler_params=pltpu.CompilerParams(
            dimension_semantics=("arbitrary", "arbitrary"),
            vmem_limit_bytes=VMEM_LIMIT),
        name="ffn",
    )(x, ffn_g, wup, cw, cb, wdn)


def _layer(x, attn_norm_g, w_in, b_forget, b_gate, moba_q_norm_g, moba_k_norm_g,
           fox_q_norm_g, fox_k_norm_g, w_branch_moba, w_branch_fox, w_out,
           ffn_norm_g, w_ffn_up, ffn_conv_w, ffn_conv_b, w_ffn_down):
    B, S, D = x.shape
    assert S % max(PROJ_TM, ATTN_T, MERGE_TM, FFN_TM) == 0 and S // MOBA_BLOCK <= AUG_ROWS
    nqkv = 6 * BRANCH_WIDTH
    scale = HEAD_DIM ** -0.5
    q_scale = scale * LOG2E

    w_bf = w_in.astype(BF16)
    w_f = jnp.pad(w_bf[:, nqkv:nqkv + N_BRANCH_HEADS], ((0, 0), (0, AUG_ROWS - N_BRANCH_HEADS)))
    wt = jnp.concatenate([w_bf[:, :nqkv], w_f], axis=1).T
    wg = w_bf[:, nqkv + N_BRANCH_HEADS:]
    gains = jnp.concatenate([moba_q_norm_g * q_scale, moba_k_norm_g,
                             fox_q_norm_g * q_scale, fox_k_norm_g])
    gains = jnp.broadcast_to(gains[:, None], (4 * HEAD_DIM, LANES)).astype(F32)
    bf = jnp.pad(b_forget.astype(F32), (0, AUG_ROWS - N_BRANCH_HEADS))
    bf = jnp.broadcast_to(bf[:, None], (AUG_ROWS, LANES))
    inv_freq = jnp.power(ROPE_THETA, -jnp.arange(ROPE_HALF, dtype=F32) * 2.0 / ROPE_DIM)
    ang = jnp.arange(S).astype(F32)[:, None] * inv_freq[None, :]
    cos_t, sin_t = jnp.cos(ang).T, jnp.sin(ang).T
    def qk_bound(gq, gk):
        return 1.01 * math.sqrt(HEAD_DIM) * jnp.max(jnp.abs(gq)) * jnp.max(jnp.abs(gk))

    thr = (F32_EXP_ZERO + 2.0 * qk_bound(fox_q_norm_g, fox_k_norm_g)).astype(F32)
    thr = jnp.broadcast_to(thr, (AUG_ROWS, LANES))
    bound2 = LOG2E * jnp.maximum(qk_bound(moba_q_norm_g, moba_k_norm_g),
                                 qk_bound(fox_q_norm_g, fox_k_norm_g))
    bound2 = (bound2 * (1.0 + 2.0 ** -7)).astype(BF16).astype(F32)
    flag = (2.0 * bound2 <= MAX_OFFSET_SPAN).astype(jnp.int32).reshape(1)
    off = jnp.broadcast_to(-bound2, (SUBLANES, LANES)).astype(F32)

    q, k, v, lo = _proj(x, attn_norm_g.reshape(1, D), wt, gains, bf, off, thr, cos_t, sin_t)
    ot = _attn(lo[:, :, 0, 0].astype(jnp.int32).reshape(-1), flag, q, k, v)
    x1 = _merge(x, ot, attn_norm_g.reshape(1, D), wg, b_gate.reshape(1, 2 * D).astype(F32),
                w_branch_moba.astype(BF16), w_branch_fox.astype(BF16), w_out.astype(BF16))
    return _ffn(x1, ffn_norm_g.reshape(1, D), w_ffn_up.astype(BF16), ffn_conv_w.astype(F32),
                ffn_conv_b.reshape(1, -1).astype(F32), w_ffn_down.astype(BF16))


def kernel(x, attn_norm_g, w_in, b_forget, b_gate, moba_q_norm_g, moba_k_norm_g,
           fox_q_norm_g, fox_k_norm_g, w_branch_moba, w_branch_fox, w_out,
           ffn_norm_g, w_ffn_up, ffn_conv_w, ffn_conv_b, w_ffn_down):
    for l in range(attn_norm_g.shape[0]):
        x = _layer(x, attn_norm_g[l], w_in[l], b_forget[l], b_gate[l], moba_q_norm_g[l],
                   moba_k_norm_g[l], fox_q_norm_g[l], fox_k_norm_g[l], w_branch_moba[l],
                   w_branch_fox[l], w_out[l], ffn_norm_g[l], w_ffn_up[l], ffn_conv_w[l],
                   ffn_conv_b[l], w_ffn_down[l])
    return x
```
